```python
import jax, jax.numpy as jnp
from jax import lax
import numpy as np

D_MODEL = 1024
BATCH = 8
SEQ = 2048
DEPTH = 2
DEC_BATCH = 128
DEC_SEQ = 4
PAST_LEN = 16384
PAGE_SIZE = 128

D_MIX = D_MODEL
A_HEADS = 4
A_HEAD_DIM = D_MIX // 16
A_WIDTH = A_HEADS * A_HEAD_DIM
A_CHUNK = 128
B_HEADS = 8
B_KEY_DIM = D_MIX // 16
B_VAL_DIM = D_MIX // 16
B_KEY_WIDTH = B_HEADS * B_KEY_DIM
B_WIDTH = B_HEADS * B_VAL_DIM
B_CHUNK = 16
C_WIDTH = D_MIX // 4
C_CONV = 31
D_FF = 4 * D_MODEL
N_IN = 2 * A_WIDTH + 2 * B_KEY_WIDTH + 2 * B_WIDTH + 2 * C_WIDTH
EPS = 1e-6

kernel_name = "hybrid_gmlp_hgrn2_conformer_adaln_step"


def _rmsnorm(x, g):
    xf = x.astype(jnp.float32)
    y = xf * lax.rsqrt(jnp.mean(xf * xf, axis=-1, keepdims=True) + EPS)
    return (y * g.astype(jnp.float32)).astype(x.dtype)


def _layernorm(x, g, b):
    xf = x.astype(jnp.float32)
    xc = xf - jnp.mean(xf, axis=-1, keepdims=True)
    y = xc * lax.rsqrt(jnp.mean(xc * xc, axis=-1, keepdims=True) + EPS)
    return (y * g.astype(jnp.float32) + b.astype(jnp.float32)).astype(x.dtype)


def _chunk_gmlp(z, ln_g, ln_b, w_s, b_s):
    n, L, _ = z.shape
    z = jax.nn.gelu(z, approximate=False)
    u, v = jnp.split(z, 2, axis=-1)
    v = _layernorm(v, ln_g, ln_b)
    lc = min(L, A_CHUNK)
    nc = L // lc
    vh = v.reshape(n, nc, lc, A_HEADS, A_HEAD_DIM)
    mask = jnp.tril(jnp.ones((lc, lc), dtype=bool))
    w = jnp.where(mask, w_s[:, :lc, :lc], 0).astype(v.dtype)
    bias = b_s[:, :lc].T[None, None, :, :, None].astype(v.dtype)
    mixed = jnp.einsum('hts,ncshd->ncthd', w, vh) + bias
    return u * mixed.reshape(n, L, A_WIDTH), v


def _hgrn2(zq, zf, zi, zg, lb, gn_g, s0):
    f32 = jnp.float32
    n, L, _ = zq.shape
    q = (jax.nn.silu(zq.astype(f32)) * B_KEY_DIM ** -0.5).reshape(n, L, B_HEADS, B_KEY_DIM)
    f = lb.astype(f32) + (1.0 - lb.astype(f32)) * jax.nn.sigmoid(zf.astype(f32))
    logf = jnp.log(f).reshape(n, L, B_HEADS, B_KEY_DIM)
    k = (1.0 - f).reshape(n, L, B_HEADS, B_KEY_DIM)
    v = zi.astype(f32).reshape(n, L, B_HEADS, B_VAL_DIM)
    nchunk = -(-L // B_CHUNK)
    pad = nchunk * B_CHUNK - L

    def to_chunks(a):
        a = jnp.pad(a, ((0, 0), (0, pad), (0, 0), (0, 0)))
        return a.reshape(n, nchunk, B_CHUNK, B_HEADS, a.shape[-1]).transpose(1, 0, 3, 2, 4)

    mask = jnp.tril(jnp.ones((B_CHUNK, B_CHUNK), dtype=bool))

    def step(S, xs):
        qc, kc, vc, lfc = xs
        b = jnp.cumsum(lfc, axis=2)
        q_in = qc * jnp.exp(b)
        k_in = kc * jnp.exp(-b)
        scores = jnp.where(mask, jnp.einsum('nhtk,nhsk->nhts', q_in, k_in), 0.0)
        o = jnp.einsum('nhtk,nhkv->nhtv', q_in, S) + jnp.einsum('nhts,nhsv->nhtv', scores, vc)
        b_last = b[:, :, -1:, :]
        S = jnp.exp(b_last[:, :, 0, :, None]) * S + jnp.einsum('nhsk,nhsv->nhkv', kc * jnp.exp(b_last - b), vc)
        return S, o

    S, o = lax.scan(step, s0.astype(f32), (to_chunks(q), to_chunks(k), to_chunks(v), to_chunks(logf)))
    o = o.transpose(1, 0, 3, 2, 4).reshape(n, nchunk * B_CHUNK, B_HEADS, B_VAL_DIM)[:, :L]
    o = _rmsnorm(o, gn_g.reshape(B_HEADS, B_VAL_DIM)).reshape(n, L, B_WIDTH)
    o = o * jax.nn.silu(zg.astype(f32))
    return o.astype(zq.dtype), S


def _conformer_conv(z, buf, w_dw, b_dw, ln_g, ln_b):
    a, gate = jnp.split(z, 2, axis=-1)
    xg = a * jax.nn.sigmoid(gate)
    xx = jnp.concatenate([buf.astype(xg.dtype), xg], axis=1)
    y = lax.conv_general_dilated(
        xx, w_dw[:, None, :].astype(xg.dtype), window_strides=(1,), padding='VALID',
        dimension_numbers=('NWC', 'WIO', 'NWC'), feature_group_count=C_WIDTH)
    y = y + b_dw.astype(xg.dtype)
    y = jax.nn.silu(_layernorm(y, ln_g, ln_b))
    return y, xx[:, -(C_CONV - 1):]


def _trunk(x, c, s_hgrn, s_conv, lb_all, w_ada, b_ada, g_pre_mix, g_post_mix, g_pre_mlp, g_post_mlp,
           w_in, a_ln_g, a_ln_b, a_w_s, a_b_s, b_gn_g, c_w_dw, c_b_dw, c_ln_g, c_ln_b,
           w_out, w_up, w_down):
    sizes = [2 * A_WIDTH, B_KEY_WIDTH, B_KEY_WIDTH, B_WIDTH, B_WIDTH, 2 * C_WIDTH]
    split_at = [int(s) for s in np.cumsum(sizes)[:-1]]
    hg_out, cv_out, v_out = [], [], []
    for l in range(DEPTH):
        mod = jax.nn.silu(c) @ w_ada[l] + b_ada[l]
        sh1, sc1, gt1, sh2, sc2, gt2 = [m[:, None, :] for m in jnp.split(mod, 6, axis=-1)]
        h = _rmsnorm(x, g_pre_mix[l]) * (1.0 + sc1) + sh1
        z = h @ w_in[l]
        za, zq, zf, zi, zg, zc = jnp.split(z, split_at, axis=-1)
        ya, v_rows = _chunk_gmlp(za, a_ln_g[l], a_ln_b[l], a_w_s[l], a_b_s[l])
        yb, S = _hgrn2(zq, zf, zi, zg, lb_all[l], b_gn_g[l], s_hgrn[l])
        yc, buf = _conformer_conv(zc, s_conv[l], c_w_dw[l], c_b_dw[l], c_ln_g[l], c_ln_b[l])
        y = jnp.concatenate([ya, yb, yc], axis=-1) @ w_out[l]
        x = x + gt1 * _rmsnorm(y, g_post_mix[l])
        h = _rmsnorm(x, g_pre_mlp[l]) * (1.0 + sc2) + sh2
        y = jnp.square(jax.nn.relu(h @ w_up[l])) @ w_down[l]
        x = x + gt2 * _rmsnorm(y, g_post_mlp[l])
        hg_out.append(S.astype(s_hgrn.dtype))
        cv_out.append(buf.astype(s_conv.dtype))
        v_out.append(v_rows)
    return x, jnp.stack(hg_out), jnp.stack(cv_out), jnp.stack(v_out)


def setup_inputs(seed: int = 0) -> dict:
    key = jax.random.key(seed)
    ks = iter(jax.random.split(key, 32))
    nrm = lambda shape, s=1.0: jax.random.normal(next(ks), shape, jnp.float32) * s
    return {
        'x_prompt': nrm((BATCH, SEQ, D_MODEL)),
        'x_sample': nrm((DEC_BATCH, DEC_SEQ, D_MODEL)),
        'state_hgrn': nrm((DEPTH, DEC_BATCH, B_HEADS, B_KEY_DIM, B_VAL_DIM), 0.5),
        'state_conv': nrm((DEPTH, DEC_BATCH, C_CONV - 1, C_WIDTH), 0.5),
        'c_prompt': nrm((BATCH, D_MODEL)),
        'c_sample': nrm((DEC_BATCH, D_MODEL)),
        'w_ada': nrm((DEPTH, D_MODEL, 6 * D_MODEL), D_MODEL ** -0.5),
        'b_ada': nrm((DEPTH, 6 * D_MODEL), 0.01),
        'g_pre_mix': 1.0 + nrm((DEPTH, D_MODEL), 0.05),
        'g_post_mix': 1.0 + nrm((DEPTH, D_MODEL), 0.05),
        'g_pre_mlp': 1.0 + nrm((DEPTH, D_MODEL), 0.05),
        'g_post_mlp': 1.0 + nrm((DEPTH, D_MODEL), 0.05),
        'w_in': nrm((DEPTH, D_MODEL, N_IN), D_MODEL ** -0.5),
        'a_ln_g': 1.0 + nrm((DEPTH, A_WIDTH), 0.05),
        'a_ln_b': nrm((DEPTH, A_WIDTH), 0.02),
        'a_w_s': nrm((DEPTH, A_HEADS, A_CHUNK, A_CHUNK), A_CHUNK ** -0.5),
        'a_b_s': 1.0 + nrm((DEPTH, A_HEADS, A_CHUNK), 0.1),
        'b_lb': nrm((DEPTH, B_KEY_WIDTH), 1.0),
        'b_gn_g': 1.0 + nrm((DEPTH, B_WIDTH), 0.05),
        'c_w_dw': nrm((DEPTH, C_CONV, C_WIDTH), C_CONV ** -0.5),
        'c_b_dw': nrm((DEPTH, C_WIDTH), 0.02),
        'c_ln_g': 1.0 + nrm((DEPTH, C_WIDTH), 0.05),
        'c_ln_b': nrm((DEPTH, C_WIDTH), 0.02),
        'w_out': nrm((DEPTH, D_MIX, D_MODEL), D_MIX ** -0.5),
        'w_up': nrm((DEPTH, D_MODEL, D_FF), D_MODEL ** -0.5),
        'w_down': nrm((DEPTH, D_FF, D_MODEL), D_FF ** -0.5),
    }


def reference(x_prompt, x_sample, state_hgrn, state_conv, c_prompt, c_sample,
              w_ada, b_ada, g_pre_mix, g_post_mix, g_pre_mlp, g_post_mlp,
              w_in, a_ln_g, a_ln_b, a_w_s, a_b_s, b_lb, b_gn_g,
              c_w_dw, c_b_dw, c_ln_g, c_ln_b, w_out, w_up, w_down):
    lb_all = jnp.cumsum(jax.nn.softmax(b_lb.astype(jnp.float32), axis=0), axis=0)
    lb_all = lb_all - lb_all[0:1]
    weights = (w_ada, b_ada, g_pre_mix, g_post_mix, g_pre_mlp, g_post_mlp,
               w_in, a_ln_g, a_ln_b, a_w_s, a_b_s, b_gn_g, c_w_dw, c_b_dw, c_ln_g, c_ln_b,
               w_out, w_up, w_down)
    nb = x_prompt.shape[0]
    hg0 = jnp.zeros((DEPTH, nb, B_HEADS, B_KEY_DIM, B_VAL_DIM), x_prompt.dtype)
    cv0 = jnp.zeros((DEPTH, nb, C_CONV - 1, C_WIDTH), x_prompt.dtype)
    y_prompt, hgrn_prompt, conv_prompt, _ = _trunk(x_prompt, c_prompt, hg0, cv0, lb_all, *weights)
    y_sample, hgrn_sample, conv_sample, gmlp_v_sample = _trunk(
        x_sample, c_sample, state_hgrn, state_conv, lb_all, *weights)
    return (y_prompt, y_sample, hgrn_prompt, hgrn_sample, conv_prompt, conv_sample, gmlp_v_sample)
```

```python
import functools

import numpy as np
import jax
import jax.numpy as jnp
from jax import lax
from jax.experimental import pallas as pl
from jax.experimental.pallas import tpu as pltpu

F32 = jnp.float32
BF16 = jnp.bfloat16

D = 1024
HEADS_A, DH_A, W_A, CHUNK_A = 4, 64, 256, 128
HEADS_B, DK, DV, W_B = 8, 64, 64, 512
SUB = 16
TILE_B = 128
W_C, TAPS = 256, 31
TAIL = 32
D_FF = 4096
N_IN = 3072
EPS = 1e-6
OFF_A, OFF_Q, OFF_F, OFF_I, OFF_G, OFF_C = 0, 512, 1024, 1536, 2048, 2560

TL = 256
TM = 256
VMEM_LIMIT = 56 * 1024 * 1024


def _dot(a, b):
    return jnp.dot(a, b, preferred_element_type=F32)


def _dot_nt(a, b):
    return lax.dot_general(a, b, (((1,), (1,)), ((), ())), preferred_element_type=F32)


def _dot_tn(a, b):
    return lax.dot_general(a, b, (((0,), (0,)), ((), ())), preferred_element_type=F32)


def _rms(x, g):
    return x * lax.rsqrt(jnp.mean(x * x, axis=-1, keepdims=True) + EPS) * g


def _ln(x, g, b):
    xc = x - jnp.mean(x, axis=-1, keepdims=True)
    return xc * lax.rsqrt(jnp.mean(xc * xc, axis=-1, keepdims=True) + EPS) * g + b


def _silu(x):
    return x * jax.nn.sigmoid(x)


def _gelu(x):
    return 0.5 * x * (1.0 + lax.erf(x * 0.7071067811865476))


def _lower_bound(blb, layer):
    m = jnp.max(blb, axis=0, keepdims=True)
    e = jnp.exp(blb - m)
    tot = jnp.sum(e, axis=0, keepdims=True)
    acc = jnp.zeros_like(tot)
    for i in range(1, layer + 1):
        acc = acc + e[i:i + 1, :]
    return acc / tot


def _group_rms(o, gavg, g):
    o2 = o * o
    hi = o2.astype(BF16)
    lo = (o2 - hi.astype(F32)).astype(BF16)
    ms = _dot(hi, gavg) + _dot(lo, gavg)
    return o * lax.rsqrt(ms + EPS) * g


def _mod_body(c_ref, w_ref, b_ref, o_ref):
    a = _silu(c_ref[...]).astype(BF16)
    o_ref[...] = _dot(a, w_ref[...].astype(BF16)) + b_ref[...]


def _mod_call(c_all, w_ada, b_ada):
    depth, _, n6 = w_ada.shape
    rows = c_all.shape[0]
    bn = 1536
    return pl.pallas_call(
        _mod_body,
        grid=(depth, n6 // bn),
        in_specs=[
            pl.BlockSpec((rows, D), lambda l, j: (0, 0)),
            pl.BlockSpec((None, D, bn), lambda l, j: (l, 0, j)),
            pl.BlockSpec((None, 1, bn), lambda l, j: (l, 0, j)),
        ],
        out_specs=pl.BlockSpec((None, rows, bn), lambda l, j: (l, 0, j)),
        out_shape=jax.ShapeDtypeStruct((depth, rows, n6), F32),
        compiler_params=pltpu.CompilerParams(
            dimension_semantics=("arbitrary", "arbitrary"), vmem_limit_bytes=VMEM_LIMIT),
        name="adaln_mod",
    )(c_all, w_ada, b_ada.reshape(depth, 1, n6))


def _gmlp_chunk(za, ln_g, ln_b, wcat_b, bias):
    z = _gelu(za)
    u = z[:, :W_A]
    v = _ln(z[:, W_A:], ln_g, ln_b)
    lane_h = lax.broadcasted_iota(jnp.int32, (CHUNK_A, W_A), 1) // DH_A
    vbd = jnp.concatenate(
        [jnp.where(lane_h == h, v, 0.0).astype(BF16) for h in range(HEADS_A)], axis=0)
    mixed = _dot(wcat_b, vbd) + bias
    return u * mixed


def _hgrn_tile(zq, zf, zi, zg, lb, gn_g, st_ref, masks_ref, bd_ref, gavg):
    nch = TILE_B // SUB
    q = _silu(zq) * (DK ** -0.5)
    f = lb + (1.0 - lb) * jax.nn.sigmoid(zf)
    lf = jnp.log(f)
    k = 1.0 - f

    r15 = lax.broadcasted_iota(jnp.int32, (TILE_B, W_B), 0) & (SUB - 1)
    cs = lf
    for d in (1, 2, 4, 8):
        cs = cs + jnp.where(r15 >= d, pltpu.roll(cs, d, 0), 0.0)
    tot = [cs[SUB * c + SUB - 1:SUB * c + SUB, :] for c in range(nch)]
    tot_b = jnp.concatenate([jnp.broadcast_to(t, (SUB, W_B)) for t in tot], axis=0)
    q_in = q * jnp.exp(cs)
    k_in = k * jnp.exp(-cs)
    k_out = k * jnp.exp(tot_b - cs)

    pre = [jnp.zeros((1, W_B), F32)]
    for c in range(nch):
        pre.append(pre[-1] + tot[c])

    def rows(fn):
        parts = []
        for c in range(nch):
            e = fn(c)
            parts.append(jnp.zeros((SUB, W_B), F32) if e is None
                         else jnp.broadcast_to(jnp.exp(e), (SUB, W_B)))
        return jnp.concatenate(parts, axis=0)

    a_st = rows(lambda c: pre[c])
    b_st = rows(lambda c: pre[nch] - pre[c + 1])
    a_64 = rows(lambda c: pre[c] - pre[4] if c >= 4 else None)
    b_64 = rows(lambda c: pre[4] - pre[c + 1] if c < 4 else None)
    a_32 = rows(lambda c: pre[c] - pre[c & ~3 | 2] if (c & 2) else None)
    b_32 = rows(lambda c: pre[c & ~3 | 2] - pre[c + 1] if not (c & 2) else None)
    decay = jnp.exp(pre[nch])

    lane = lax.broadcasted_iota(jnp.int32, (TILE_B, 128), 1)
    m0 = lane < DK

    def stack2(x):
        return jnp.concatenate([jnp.where(m0, x, 0.0).astype(BF16),
                                jnp.where(m0, 0.0, x).astype(BF16)], axis=0)

    levels = ((q_in, k_in), (q_in, k_out), (q_in * a_32, k_out * b_32), (q_in * a_64, k_out * b_64))
    q_st = q_in * a_st
    k_st = k_out * b_st
    bd = bd_ref[...] > 0.5

    outs = []
    for p in range(HEADS_B // 2):
        sl = slice(128 * p, 128 * p + 128)
        att = jnp.zeros((TILE_B, 2 * TILE_B), F32)
        for lv, (qa, ka) in enumerate(levels):
            s = _dot_nt(qa[:, sl].astype(BF16), stack2(ka[:, sl]))
            att = att + jnp.where(masks_ref[lv] > 0.5, s, 0.0)
        v_p = zi[:, sl]
        o = _dot(att.astype(BF16), stack2(v_p))
        st = st_ref[p]
        o = o + _dot_nt(q_st[:, sl].astype(BF16), st.astype(BF16))
        upd = _dot_tn(v_p.astype(BF16), k_st[:, sl].astype(BF16))
        st_ref[p] = st * decay[:, sl] + jnp.where(bd, upd, 0.0)
        outs.append(o)
    o = jnp.concatenate(outs, axis=1)
    return _group_rms(o, gavg, gn_g) * _silu(zg)


def _mixp_body(layer,
               x_ref, sh_ref, sc_ref, gt_ref, gpre_ref, gpost_ref, win_ref, wout_ref,
               alg_ref, alb_ref, wcat_ref, abias_ref, blb_ref, gng_ref,
               cw_ref, cb_ref, clg_ref, clb_ref, masks_ref, bd_ref, gavg_ref,
               o_ref, st_ref, tail_ref,
               z_ref, xx_ref, y_ref):
    j = pl.program_id(1)

    @pl.when(j == 0)
    def _():
        st_ref[...] = jnp.zeros_like(st_ref)
        tail_ref[...] = jnp.zeros_like(tail_ref)

    x = x_ref[...]
    h = _rms(x, gpre_ref[...]) * (1.0 + sc_ref[...]) + sh_ref[...]
    z_ref[...] = _dot(h.astype(BF16), win_ref[...])

    tri = (lax.broadcasted_iota(jnp.int32, (CHUNK_A, HEADS_A * CHUNK_A), 1) & (CHUNK_A - 1)
           ) <= lax.broadcasted_iota(jnp.int32, (CHUNK_A, HEADS_A * CHUNK_A), 0)
    wcat_b = jnp.where(tri, wcat_ref[...], 0.0).astype(BF16)
    for c in range(TL // CHUNK_A):
        r = slice(c * CHUNK_A, (c + 1) * CHUNK_A)
        ya = _gmlp_chunk(z_ref[r, OFF_A:OFF_Q], alg_ref[...], alb_ref[...], wcat_b, abias_ref[...])
        y_ref[r, 0:W_A] = ya.astype(BF16)

    lb = _lower_bound(blb_ref[...], layer)
    gavg = gavg_ref[...]
    for t in range(TL // TILE_B):
        r = slice(t * TILE_B, (t + 1) * TILE_B)
        yb = _hgrn_tile(z_ref[r, OFF_Q:OFF_F], z_ref[r, OFF_F:OFF_I], z_ref[r, OFF_I:OFF_G],
                        z_ref[r, OFF_G:OFF_C], lb, gng_ref[...], st_ref, masks_ref, bd_ref, gavg)
        y_ref[r, W_A:W_A + W_B] = yb.astype(BF16)

    xg = z_ref[:, OFF_C:OFF_C + W_C] * jax.nn.sigmoid(z_ref[:, OFF_C + W_C:N_IN])
    xx_ref[0:TAIL, :] = tail_ref[...]
    xx_ref[TAIL:TAIL + TL, :] = xg
    tail_ref[...] = xg[TL - TAIL:TL, :]
    rb = 64
    for c in range(TL // rb):
        acc = jnp.zeros((rb, W_C), F32)
        for t in range(TAPS):
            acc = acc + xx_ref[pl.ds(c * rb + TAIL - (TAPS - 1) + t, rb), :] * cw_ref[pl.ds(t, 1), :]
        yc = _silu(_ln(acc + cb_ref[...], clg_ref[...], clb_ref[...]))
        y_ref[c * rb:(c + 1) * rb, W_A + W_B:D] = yc.astype(BF16)

    y = _dot(y_ref[...], wout_ref[...])
    o_ref[...] = x + gt_ref[...] * _rms(y, gpost_ref[...])


def _level_masks():
    t = np.arange(TILE_B)[:, None]
    s = np.arange(2 * TILE_B)[None, :] % TILE_B
    ct, cs = t // SUB, s // SUB
    m = [
        (ct == cs) & (s <= t),
        (ct == cs + 1) & (ct % 2 == 1),
        (ct // 2 == cs // 2 + 1) & ((ct // 2) % 2 == 1),
        (ct // 4 == cs // 4 + 1) & ((ct // 4) % 2 == 1),
    ]
    return np.stack(m).astype(np.float32)


def _block_diag_mask():
    i = np.arange(128)
    return ((i[:, None] // DK) == (i[None, :] // DK)).astype(np.float32)


def _group_avg():
    i = np.arange(W_B)
    return ((i[:, None] // DV) == (i[None, :] // DV)).astype(np.float32) / DV


def _full(shape):
    nd = len(shape)
    return pl.BlockSpec(shape, lambda *_: (0,) * nd)


def _mixp_call(layer, x, modp, gpre, gpost, win_b, wout_b, alg, alb, wcat, abias, blb, gng,
               cw, cb, clg, clb, masks, bdm, gavg):
    nb, L, _ = x.shape
    row = lambda a: a.reshape(1, -1)
    mspec = lambda kk: pl.BlockSpec((None, 1, D), lambda n, j: (n, 0, kk))
    return pl.pallas_call(
        functools.partial(_mixp_body, layer),
        grid=(nb, L // TL),
        in_specs=[
            pl.BlockSpec((None, TL, D), lambda n, j: (n, j, 0)),
            mspec(0), mspec(1), mspec(2),
            _full((1, D)), _full((1, D)), _full((D, N_IN)), _full((D, D)),
            _full((1, W_A)), _full((1, W_A)), _full((CHUNK_A, HEADS_A * CHUNK_A)), _full((CHUNK_A, W_A)),
            _full(blb.shape), _full((1, W_B)),
            _full((TAPS, W_C)), _full((1, W_C)), _full((1, W_C)), _full((1, W_C)),
            _full(masks.shape), _full(bdm.shape), _full(gavg.shape),
        ],
        out_specs=[
            pl.BlockSpec((None, TL, D), lambda n, j: (n, j, 0)),
            pl.BlockSpec((None, HEADS_B // 2, 128, 128), lambda n, j: (n, 0, 0, 0)),
            pl.BlockSpec((None, TAIL, W_C), lambda n, j: (n, 0, 0)),
        ],
        out_shape=[
            jax.ShapeDtypeStruct((nb, L, D), F32),
            jax.ShapeDtypeStruct((nb, HEADS_B // 2, 128, 128), F32),
            jax.ShapeDtypeStruct((nb, TAIL, W_C), F32),
        ],
        scratch_shapes=[
            pltpu.VMEM((TL, N_IN), F32),
            pltpu.VMEM((TAIL + TL, W_C), F32),
            pltpu.VMEM((TL, D), BF16),
        ],
        compiler_params=pltpu.CompilerParams(
            dimension_semantics=("arbitrary", "arbitrary"), vmem_limit_bytes=VMEM_LIMIT),
        name=f"mix_prompt_l{layer}",
    )(x, modp, modp, modp, row(gpre), row(gpost), win_b, wout_b, row(alg), row(alb), wcat, abias,
      blb, row(gng), cw, row(cb), row(clg), row(clb), masks, bdm, gavg)


def _mixs_body(layer, nt,
               x_ref, sh_ref, sc_ref, gt_ref, gpre_ref, gpost_ref, win_ref, wout_ref,
               alg_ref, alb_ref, acoef_ref, abias_ref, blb_ref, gng_ref,
               cw_ref, cb_ref, clg_ref, clb_ref, gavg_ref, cst_ref, s_ref,
               o_ref, so_ref, xg_ref, v_ref,
               z_ref, y_ref, ft_ref, kt_ref, qt_ref, vt_ref, ot_ref):
    hd = pl.program_id(0)
    ns = x_ref.shape[1]

    @pl.when(hd == 0)
    def _():
        x = x_ref[...]
        h = _rms(x, gpre_ref[...]) * (1.0 + sc_ref[...]) + sh_ref[...]
        z_ref[...] = _dot(h.reshape(nt * ns, D).astype(BF16), win_ref[...])

        za = _gelu(z_ref[:, OFF_A:OFF_Q])
        v = _ln(za[:, W_A:], alg_ref[...], alb_ref[...])
        for t in range(nt):
            mixed = abias_ref[pl.ds(t, 1), :]
            for s in range(t + 1):
                mixed = mixed + acoef_ref[pl.ds(t * nt + s, 1), :] * v[s * ns:(s + 1) * ns, :]
            y_ref[t * ns:(t + 1) * ns, 0:W_A] = (za[t * ns:(t + 1) * ns, :W_A] * mixed).astype(BF16)
            v_ref[t] = v[t * ns:(t + 1) * ns, :]

        xg = z_ref[:, OFF_C:OFF_C + W_C] * jax.nn.sigmoid(z_ref[:, OFF_C + W_C:N_IN])
        hist = TAPS - 1
        for t in range(nt):
            xg_ref[t] = xg[t * ns:(t + 1) * ns, :]
        for t in range(nt):
            acc = jnp.zeros((ns, W_C), F32)
            for tap in range(TAPS):
                pos = t + tap
                src = cst_ref[pos] if pos < hist else xg[(pos - hist) * ns:(pos - hist + 1) * ns, :]
                acc = acc + src * cw_ref[pl.ds(tap, 1), :]
            yc = _silu(_ln(acc + cb_ref[...], clg_ref[...], clb_ref[...]))
            y_ref[t * ns:(t + 1) * ns, W_A + W_B:D] = yc.astype(BF16)

        lb = _lower_bound(blb_ref[...], layer)
        for t in range(nt):
            r = slice(t * ns, (t + 1) * ns)
            f = lb + (1.0 - lb) * jax.nn.sigmoid(z_ref[r, OFF_F:OFF_I])
            ft_ref[t] = f.T
            kt_ref[t] = (1.0 - f).T
            qt_ref[t] = (_silu(z_ref[r, OFF_Q:OFF_F]) * (DK ** -0.5)).T
            vt_ref[t] = z_ref[r, OFF_I:OFF_G].T

    base = pl.multiple_of(hd * DK, DK)
    vts = [vt_ref[t, pl.ds(base, DV), :] for t in range(nt)]

    def krow(i, accs):
        srow = s_ref[i]
        new = []
        for t in range(nt):
            srow = ft_ref[t, pl.ds(base + i, 1), :] * srow + kt_ref[t, pl.ds(base + i, 1), :] * vts[t]
            new.append(accs[t] + qt_ref[t, pl.ds(base + i, 1), :] * srow)
        so_ref[i] = srow
        return tuple(new)

    accs = lax.fori_loop(0, DK, krow, tuple(jnp.zeros((DV, ns), F32) for _ in range(nt)))
    for t in range(nt):
        ot_ref[t, pl.ds(base, DV), :] = accs[t]

    @pl.when(hd == HEADS_B - 1)
    def _():
        gavg = gavg_ref[...]
        for t in range(nt):
            r = slice(t * ns, (t + 1) * ns)
            o = ot_ref[t].T
            yb = _group_rms(o, gavg, gng_ref[...]) * _silu(z_ref[r, OFF_G:OFF_C])
            y_ref[r, W_A:W_A + W_B] = yb.astype(BF16)
        y = _dot(y_ref[...], wout_ref[...]).reshape(nt, ns, D)
        o_ref[...] = x_ref[...] + gt_ref[...] * _rms(y, gpost_ref[...])


def _mixs_call(layer, xs, mods, gpre, gpost, win_b, wout_b, alg, alb, acoef, abias, blb, gng,
               cw, cb, clg, clb, gavg, cst, s_in):
    nt, ns, _ = xs.shape
    row = lambda a: a.reshape(1, -1)
    mspec = lambda kk: pl.BlockSpec((ns, D), lambda h: (0, kk))
    hist = TAPS - 1
    return pl.pallas_call(
        functools.partial(_mixs_body, layer, nt),
        grid=(HEADS_B,),
        in_specs=[
            _full((nt, ns, D)),
            mspec(0), mspec(1), mspec(2),
            _full((1, D)), _full((1, D)), _full((D, N_IN)), _full((D, D)),
            _full((1, W_A)), _full((1, W_A)), _full(acoef.shape), _full(abias.shape),
            _full(blb.shape), _full((1, W_B)),
            _full((TAPS, W_C)), _full((1, W_C)), _full((1, W_C)), _full((1, W_C)),
            _full(gavg.shape), _full((hist, ns, W_C)),
            pl.BlockSpec((None, DK, DV, ns), lambda h: (h, 0, 0, 0)),
        ],
        out_specs=[
            _full((nt, ns, D)),
            pl.BlockSpec((None, DK, DV, ns), lambda h: (h, 0, 0, 0)),
            _full((nt, ns, W_C)),
            _full((nt, ns, W_A)),
        ],
        out_shape=[
            jax.ShapeDtypeStruct((nt, ns, D), F32),
            jax.ShapeDtypeStruct((HEADS_B, DK, DV, ns), F32),
            jax.ShapeDtypeStruct((nt, ns, W_C), F32),
            jax.ShapeDtypeStruct((nt, ns, W_A), F32),
        ],
        scratch_shapes=[
            pltpu.VMEM((nt * ns, N_IN), F32),
            pltpu.VMEM((nt * ns, D), BF16),
            pltpu.VMEM((nt, W_B, ns), F32),
            pltpu.VMEM((nt, W_B, ns), F32),
            pltpu.VMEM((nt, W_B, ns), F32),
            pltpu.VMEM((nt, W_B, ns), F32),
            pltpu.VMEM((nt, W_B, ns), F32),
        ],
        compiler_params=pltpu.CompilerParams(
            dimension_semantics=("arbitrary",), vmem_limit_bytes=VMEM_LIMIT),
        name=f"mix_sample_l{layer}",
    )(xs, mods, mods, mods, row(gpre), row(gpost), win_b, wout_b, row(alg), row(alb), acoef, abias,
      blb, row(gng), cw, row(cb), row(clg), row(clb), gavg, cst, s_in)


def _mlp_body(x_ref, sh_ref, sc_ref, gt_ref, gpre_ref, gpost_ref, wup_ref, wdn_ref, o_ref):
    x = x_ref[...]
    a, b, _ = x.shape
    h = (_rms(x, gpre_ref[...]) * (1.0 + sc_ref[...]) + sh_ref[...]).reshape(a * b, D).astype(BF16)
    cw = 1024
    y = jnp.zeros((a * b, D), F32)
    for c in range(D_FF // cw):
        u = jnp.maximum(_dot(h, wup_ref[:, c * cw:(c + 1) * cw]), 0.0)
        y = y + _dot((u * u).astype(BF16), wdn_ref[c * cw:(c + 1) * cw, :])
    o_ref[...] = x + gt_ref[...] * _rms(y, gpost_ref[...]).reshape(a, b, D)


def _mlp_call(name, x, mod, gpre, gpost, wup_b, wdn_b, per_seq):
    row = lambda v: v.reshape(1, -1)
    if per_seq:
        n, L, _ = x.shape
        grid = (n, L // TM)
        xspec = pl.BlockSpec((1, TM, D), lambda i, j: (i, j, 0))
        mspec = lambda kk: pl.BlockSpec((1, 1, D), lambda i, j: (i, 0, kk))
    else:
        t, n, _ = x.shape
        grid = (1, 1)
        xspec = pl.BlockSpec((t, n, D), lambda i, j: (0, 0, 0))
        mspec = lambda kk: pl.BlockSpec((1, n, D), lambda i, j: (0, 0, kk))
    return pl.pallas_call(
        _mlp_body,
        grid=grid,
        in_specs=[xspec, mspec(3), mspec(4), mspec(5),
                  pl.BlockSpec((1, D), lambda i, j: (0, 0)), pl.BlockSpec((1, D), lambda i, j: (0, 0)),
                  pl.BlockSpec((D, D_FF), lambda i, j: (0, 0)), pl.BlockSpec((D_FF, D), lambda i, j: (0, 0))],
        out_specs=xspec,
        out_shape=jax.ShapeDtypeStruct(x.shape, F32),
        compiler_params=pltpu.CompilerParams(
            dimension_semantics=("arbitrary", "arbitrary"), vmem_limit_bytes=VMEM_LIMIT),
        name=name,
    )(x, mod, mod, mod, row(gpre), row(gpost), wup_b, wdn_b)


def kernel(x_prompt, x_sample, state_hgrn, state_conv, c_prompt, c_sample, w_ada, b_ada, g_pre_mix, g_post_mix, g_pre_mlp, g_post_mlp, w_in, a_ln_g, a_ln_b, a_w_s, a_b_s, b_lb, b_gn_g, c_w_dw, c_b_dw, c_ln_g, c_ln_b, w_out, w_up, w_down):
    depth = w_in.shape[0]
    nb = x_prompt.shape[0]
    ns, nt, _ = x_sample.shape
    hist = TAPS - 1

    mod = _mod_call(jnp.concatenate([c_sample, c_prompt], axis=0), w_ada, b_ada)
    masks = jnp.asarray(_level_masks())
    bdm = jnp.asarray(_block_diag_mask())
    gavg = jnp.asarray(_group_avg(), dtype=BF16)

    xp = x_prompt
    xs = jnp.transpose(x_sample, (1, 0, 2))
    hg_p, hg_s, cv_p, cv_s, v_s = [], [], [], [], []
    for l in range(depth):
        win_b, wout_b = w_in[l].astype(BF16), w_out[l].astype(BF16)
        wup_b, wdn_b = w_up[l].astype(BF16), w_down[l].astype(BF16)
        mods = mod[l, :ns]
        modp = mod[l, ns:].reshape(nb, 1, 6 * D)
        wcat = jnp.transpose(a_w_s[l], (1, 0, 2)).reshape(CHUNK_A, HEADS_A * CHUNK_A)
        abias = jnp.repeat(a_b_s[l].T, DH_A, axis=1)
        acoef = jnp.repeat(jnp.transpose(a_w_s[l][:, :nt, :nt], (1, 2, 0)).reshape(nt * nt, HEADS_A),
                           DH_A, axis=1)

        xp, st_p, tail_p = _mixp_call(
            l, xp, modp, g_pre_mix[l], g_post_mix[l], win_b, wout_b, a_ln_g[l], a_ln_b[l], wcat, abias,
            b_lb, b_gn_g[l], c_w_dw[l], c_b_dw[l], c_ln_g[l], c_ln_b[l], masks, bdm, gavg)
        xp = _mlp_call(f"mlp_prompt_l{l}", xp, modp, g_pre_mlp[l], g_post_mlp[l], wup_b, wdn_b, True)

        cst = jnp.transpose(state_conv[l], (1, 0, 2))
        s_in = jnp.transpose(state_hgrn[l], (1, 2, 3, 0))
        xs, s_out, xg_s, vrow = _mixs_call(
            l, xs, mods, g_pre_mix[l], g_post_mix[l], win_b, wout_b, a_ln_g[l], a_ln_b[l], acoef,
            abias[:nt], b_lb, b_gn_g[l], c_w_dw[l], c_b_dw[l], c_ln_g[l], c_ln_b[l], gavg, cst, s_in)
        xs = _mlp_call(f"mlp_sample_l{l}", xs, mods.reshape(1, ns, 6 * D), g_pre_mlp[l], g_post_mlp[l],
                       wup_b, wdn_b, False)

        st6 = st_p.reshape(nb, HEADS_B // 2, 2, DV, 2, DK)
        hg = jnp.stack([st6[:, :, 0, :, 0, :], st6[:, :, 1, :, 1, :]], axis=2)
        hg_p.append(jnp.transpose(hg.reshape(nb, HEADS_B, DV, DK), (0, 1, 3, 2)))
        hg_s.append(jnp.transpose(s_out, (3, 0, 1, 2)))
        cv_p.append(tail_p[:, TAIL - hist:, :])
        cv_s.append(jnp.concatenate([state_conv[l][:, nt:, :], jnp.transpose(xg_s, (1, 0, 2))], axis=1))
        v_s.append(jnp.transpose(vrow, (1, 0, 2)))

    return (xp, jnp.transpose(xs, (1, 0, 2)), jnp.stack(hg_p), jnp.stack(hg_s),
            jnp.stack(cv_p), jnp.stack(cv_s), jnp.stack(v_s))
```

```python
import functools

import numpy as np
import jax
import jax.numpy as jnp
from jax import lax
from jax.experimental import pallas as pl
from jax.experimental.pallas import tpu as pltpu

F32 = jnp.float32
BF16 = jnp.bfloat16

D = 1024
HEADS_A, DH_A, W_A, CHUNK_A = 4, 64, 256, 128
HEADS_B, DK, DV, W_B = 8, 64, 64, 512
SUB = 16
TILE_B = 128
W_C, TAPS = 256, 31
TAIL = 32
D_FF = 4096
N_IN = 3072
EPS = 1e-6
OFF_A, OFF_Q, OFF_F, OFF_I, OFF_G, OFF_C = 0, 512, 1024, 1536, 2048, 2560

TL = 256
TM = 256
VMEM_LIMIT = 56 * 1024 * 1024


def _dot(a, b):
    return jnp.dot(a, b, preferred_element_type=F32)


def _dot_nt(a, b):
    return lax.dot_general(a, b, (((1,), (1,)), ((), ())), preferred_element_type=F32)


def _dot_tn(a, b):
    return lax.dot_general(a, b, (((0,), (0,)), ((), ())), preferred_element_type=F32)


def _rms(x, g):
    return x * lax.rsqrt(jnp.mean(x * x, axis=-1, keepdims=True) + EPS) * g


def _ln(x, g, b):
    xc = x - jnp.mean(x, axis=-1, keepdims=True)
    return xc * lax.rsqrt(jnp.mean(xc * xc, axis=-1, keepdims=True) + EPS) * g + b


def _silu(x):
    return x * jax.nn.sigmoid(x)


def _gelu(x):
    return 0.5 * x * (1.0 + lax.erf(x * 0.7071067811865476))


def _lower_bound(blb, layer):
    m = jnp.max(blb, axis=0, keepdims=True)
    e = jnp.exp(blb - m)
    tot = jnp.sum(e, axis=0, keepdims=True)
    acc = jnp.zeros_like(tot)
    for i in range(1, layer + 1):
        acc = acc + e[i:i + 1, :]
    return acc / tot


def _group_rms(o, gavg, g):
    o2 = o * o
    hi = o2.astype(BF16)
    lo = (o2 - hi.astype(F32)).astype(BF16)
    ms = _dot(hi, gavg) + _dot(lo, gavg)
    return o * lax.rsqrt(ms + EPS) * g


def _mod_body(c_ref, w_ref, b_ref, o_ref):
    a = _silu(c_ref[...]).astype(BF16)
    o_ref[...] = _dot(a, w_ref[...].astype(BF16)) + b_ref[...]


def _mod_call(c_all, w_ada, b_ada):
    depth, _, n6 = w_ada.shape
    rows = c_all.shape[0]
    bn = 1536
    return pl.pallas_call(
        _mod_body,
        grid=(depth, n6 // bn),
        in_specs=[
            pl.BlockSpec((rows, D), lambda l, j: (0, 0)),
            pl.BlockSpec((None, D, bn), lambda l, j: (l, 0, j)),
            pl.BlockSpec((None, 1, bn), lambda l, j: (l, 0, j)),
        ],
        out_specs=pl.BlockSpec((None, rows, bn), lambda l, j: (l, 0, j)),
        out_shape=jax.ShapeDtypeStruct((depth, rows, n6), F32),
        compiler_params=pltpu.CompilerParams(
            dimension_semantics=("arbitrary", "arbitrary"), vmem_limit_bytes=VMEM_LIMIT),
        name="adaln_mod",
    )(c_all, w_ada, b_ada.reshape(depth, 1, n6))


def _gmlp_chunk(za, ln_g, ln_b, wcat_b, bias):
    z = _gelu(za)
    u = z[:, :W_A]
    v = _ln(z[:, W_A:], ln_g, ln_b)
    lane_h = lax.broadcasted_iota(jnp.int32, (CHUNK_A, W_A), 1) // DH_A
    vbd = jnp.concatenate(
        [jnp.where(lane_h == h, v, 0.0).astype(BF16) for h in range(HEADS_A)], axis=0)
    mixed = _dot(wcat_b, vbd) + bias
    return u * mixed


def _hgrn_tile(zq, zf, zi, zg, lb, gn_g, st_ref, masks_ref, lmask_ref, bd_ref, gavg, keep):
    nch = TILE_B // SUB
    q = _silu(zq) * (DK ** -0.5)
    f = lb + (1.0 - lb) * jax.nn.sigmoid(zf)
    lf = jnp.log(f)
    k = 1.0 - f

    r15 = lax.broadcasted_iota(jnp.int32, (TILE_B, W_B), 0) & (SUB - 1)
    cs = lf
    for d in (1, 2, 4, 8):
        cs = cs + jnp.where(r15 >= d, pltpu.roll(cs, d, 0), 0.0)
    tot = [cs[SUB * c + SUB - 1:SUB * c + SUB, :] for c in range(nch)]
    tot_b = jnp.concatenate([jnp.broadcast_to(t, (SUB, W_B)) for t in tot], axis=0)
    q_in = q * jnp.exp(cs)
    k_out = k * jnp.exp(tot_b - cs)

    q0, k0 = [], []
    for c in range(nch):
        r = slice(SUB * c, SUB * c + SUB)
        if c % 2 == 0:
            q0.append(q[r] * jnp.exp(cs[r] - tot[c]))
            k0.append(k_out[r])
        else:
            q0.append(q_in[r])
            k0.append(k[r] * jnp.exp(-cs[r]))
    q0 = jnp.concatenate(q0, axis=0)
    k0 = jnp.concatenate(k0, axis=0)

    pre = [jnp.zeros((1, W_B), F32)]
    for c in range(nch):
        pre.append(pre[-1] + tot[c])

    def rows(fn):
        parts = []
        for c in range(nch):
            e = fn(c)
            parts.append(jnp.zeros((SUB, W_B), F32) if e is None
                         else jnp.broadcast_to(jnp.exp(e), (SUB, W_B)))
        return jnp.concatenate(parts, axis=0)

    a_st = rows(lambda c: pre[c])
    b_st = rows(lambda c: pre[nch] - pre[c + 1])
    a_64 = rows(lambda c: pre[c] - pre[4] if c >= 4 else None)
    b_64 = rows(lambda c: pre[4] - pre[c + 1] if c < 4 else None)
    a_32 = rows(lambda c: pre[c] - pre[c & ~3 | 2] if (c & 2) else None)
    b_32 = rows(lambda c: pre[c & ~3 | 2] - pre[c + 1] if not (c & 2) else None)
    decay = jnp.exp(pre[nch])

    m0 = lmask_ref[...] > 0.5
    zb = jnp.zeros((TILE_B, 128), BF16)

    def stack2(x):
        xb = x.astype(BF16)
        return jnp.concatenate([jnp.where(m0, xb, zb), jnp.where(m0, zb, xb)], axis=0)

    levels = ((q0, k0), (q_in * a_32, k_out * b_32), (q_in * a_64, k_out * b_64))
    q_st = q_in * a_st
    k_st = k_out * b_st
    v_t = zi.T
    bd = jnp.logical_and(bd_ref[...] > 0.5, keep)
    decay = jnp.where(keep, decay, 1.0)
    mk = [masks_ref[lv] > 0.5 for lv in range(len(levels))]

    outs = []
    for p in range(HEADS_B // 2):
        sl = slice(128 * p, 128 * p + 128)
        att = jnp.zeros((TILE_B, 2 * TILE_B), BF16)
        for lv, (qa, ka) in reversed(list(enumerate(levels))):
            s = _dot_nt(qa[:, sl].astype(BF16), stack2(ka[:, sl]))
            att = jnp.where(mk[lv], s.astype(BF16), att)
        o = _dot(att, stack2(zi[:, sl]))
        st = st_ref[p]
        o = o + _dot_nt(q_st[:, sl].astype(BF16), st.astype(BF16))
        upd = _dot(v_t[sl, :].astype(BF16), k_st[:, sl].astype(BF16))
        st_ref[p] = st * decay[:, sl] + jnp.where(bd, upd, 0.0)
        outs.append(o)
    o = jnp.concatenate(outs, axis=1)
    return _group_rms(o, gavg, gn_g) * _silu(zg)


def _layerp_body(layer, n_l, n_tiles,
                 x_ref, sh_ref, sc_ref, gt_ref, sh2_ref, sc2_ref, gt2_ref,
                 gpre_ref, gpost_ref, gpre2_ref, gpost2_ref, win_ref, wout_ref, wup_ref, wdn_ref,
                 alg_ref, alb_ref, wcat_ref, abias_ref, blb_ref, gng_ref,
                 cw_ref, cb_ref, clg_ref, clb_ref, masks_ref, lmask_ref, bd_ref, gavg_ref,
                 o_ref, st_ref, tail_ref,
                 z_ref, xx_ref, xs_ref, y_ref, x1_ref):
    i = pl.program_id(0)
    keep = i < n_tiles

    @pl.when(jnp.logical_and(i % n_l == 0, keep))
    def _():
        st_ref[...] = jnp.zeros_like(st_ref)
        tail_ref[...] = jnp.zeros_like(tail_ref)

    @pl.when(i == 0)
    def _():
        x1_ref[1] = jnp.zeros((TL, D), F32)

    xm = x1_ref[(i + 1) % 2]
    hm = (_rms(xm, gpre2_ref[...] * (1.0 + sc2_ref[...])) + sh2_ref[...]).astype(BF16)
    cwid = D_FF // 4

    def mlp_chunk(c, acc):
        u = jnp.maximum(_dot(hm, wup_ref[:, c * cwid:(c + 1) * cwid]), 0.0)
        part = _dot((u * u).astype(BF16), wdn_ref[c * cwid:(c + 1) * cwid, :])
        return part if acc is None else acc + part

    x = x_ref[...]
    h = _rms(x, gpre_ref[...] * (1.0 + sc_ref[...])) + sh_ref[...]
    z_ref[...] = _dot(h.astype(BF16), win_ref[...])

    ym = mlp_chunk(0, None)

    xg = z_ref[:, OFF_C:OFF_C + W_C] * jax.nn.sigmoid(z_ref[:, OFF_C + W_C:N_IN])
    old_tail = tail_ref[...]
    xx_ref[0:TAIL, :] = old_tail
    xx_ref[TAIL:TAIL + TL, :] = xg
    tail_ref[...] = jnp.where(keep, xg[TL - TAIL:TL, :], old_tail)
    span = TL + TAIL - 8
    for r in range(1, 8):
        xs_ref[r - 1] = xx_ref[pl.ds(r, span), :]
    rb = 64
    for c in range(TL // rb):
        acc = jnp.zeros((rb, W_C), F32)
        for t in range(TAPS):
            a, r = divmod(TAIL - (TAPS - 1) + t, 8)
            src = xx_ref[pl.ds(c * rb + 8 * a, rb), :] if r == 0 else xs_ref[r - 1, pl.ds(c * rb + 8 * a, rb), :]
            acc = acc + src * cw_ref[pl.ds(t, 1), :]
        yc = _silu(_ln(acc + cb_ref[...], clg_ref[...], clb_ref[...]))
        y_ref[c * rb:(c + 1) * rb, W_A + W_B:D] = yc.astype(BF16)

    lb = _lower_bound(blb_ref[...], layer)
    gavg = gavg_ref[...]
    for t in range(TL // TILE_B):
        ym = mlp_chunk(1 + t, ym)
        r = slice(t * TILE_B, (t + 1) * TILE_B)
        yb = _hgrn_tile(z_ref[r, OFF_Q:OFF_F], z_ref[r, OFF_F:OFF_I], z_ref[r, OFF_I:OFF_G],
                        z_ref[r, OFF_G:OFF_C], lb, gng_ref[...], st_ref, masks_ref, lmask_ref, bd_ref, gavg,
                        keep)
        y_ref[r, W_A:W_A + W_B] = yb.astype(BF16)

    ym = mlp_chunk(3, ym)

    tri = (lax.broadcasted_iota(jnp.int32, (CHUNK_A, HEADS_A * CHUNK_A), 1) & (CHUNK_A - 1)
           ) <= lax.broadcasted_iota(jnp.int32, (CHUNK_A, HEADS_A * CHUNK_A), 0)
    wcat_b = jnp.where(tri, wcat_ref[...], 0.0).astype(BF16)
    for c in range(TL // CHUNK_A):
        r = slice(c * CHUNK_A, (c + 1) * CHUNK_A)
        ya = _gmlp_chunk(z_ref[r, OFF_A:OFF_Q], alg_ref[...], alb_ref[...], wcat_b, abias_ref[...])
        y_ref[r, 0:W_A] = ya.astype(BF16)

    o_ref[...] = xm + _rms(ym, gt2_ref[...] * gpost2_ref[...])
    y = _dot(y_ref[...], wout_ref[...])
    x1_ref[i % 2] = x + _rms(y, gt_ref[...] * gpost_ref[...])


def _level_masks():
    t = np.arange(TILE_B)[:, None]
    s = np.arange(2 * TILE_B)[None, :] % TILE_B
    ct, cs = t // SUB, s // SUB
    m = [
        (ct // 2 == cs // 2) & (s <= t),
        (ct // 2 == cs // 2 + 1) & ((ct // 2) % 2 == 1),
        (ct // 4 == cs // 4 + 1) & ((ct // 4) % 2 == 1),
    ]
    return np.stack(m).astype(np.float32)


def _first_head_lanes():
    return np.broadcast_to(np.arange(128)[None, :] < DK, (TILE_B, 128)).astype(np.float32)


def _block_diag_mask():
    i = np.arange(128)
    return ((i[:, None] // DK) == (i[None, :] // DK)).astype(np.float32)


def _group_avg():
    i = np.arange(W_B)
    return ((i[:, None] // DV) == (i[None, :] // DV)).astype(np.float32) / DV


def _full(shape):
    nd = len(shape)
    return pl.BlockSpec(shape, lambda *_: (0,) * nd)


def _layerp_call(layer, x, modp, gpre, gpost, gpre2, gpost2, win_b, wout_b, wup_b, wdn_b,
                 alg, alb, wcat, abias, blb, gng, cw, cb, clg, clb, masks, lmask, bdm, gavg):
    nb, L, _ = x.shape
    n_l = L // TL
    n_tiles = nb * n_l
    row = lambda a: a.reshape(1, -1)
    cur = lambda i: jnp.minimum(i, n_tiles - 1)
    prv = lambda i: jnp.maximum(i - 1, 0)
    mcur = lambda kk: pl.BlockSpec((None, 1, D), lambda i: (cur(i) // n_l, 0, kk))
    mprv = lambda kk: pl.BlockSpec((None, 1, D), lambda i: (prv(i) // n_l, 0, kk))
    once = lambda shape: pl.BlockSpec(shape, lambda i: (0,) * len(shape), pipeline_mode=pl.Buffered(1))
    return pl.pallas_call(
        functools.partial(_layerp_body, layer, n_l, n_tiles),
        grid=(n_tiles + 1,),
        in_specs=[
            pl.BlockSpec((None, TL, D), lambda i: (cur(i) // n_l, cur(i) % n_l, 0)),
            mcur(0), mcur(1), mcur(2), mprv(3), mprv(4), mprv(5),
            _full((1, D)), _full((1, D)), _full((1, D)), _full((1, D)),
            once((D, N_IN)), once((D, D)), once((D, D_FF)), once((D_FF, D)),
            _full((1, W_A)), _full((1, W_A)), _full((CHUNK_A, HEADS_A * CHUNK_A)), _full((CHUNK_A, W_A)),
            _full(blb.shape), _full((1, W_B)),
            _full((TAPS, W_C)), _full((1, W_C)), _full((1, W_C)), _full((1, W_C)),
            _full(masks.shape), _full(lmask.shape), _full(bdm.shape), _full(gavg.shape),
        ],
        out_specs=[
            pl.BlockSpec((None, TL, D), lambda i: (prv(i) // n_l, prv(i) % n_l, 0)),
            pl.BlockSpec((None, HEADS_B // 2, 128, 128), lambda i: (cur(i) // n_l, 0, 0, 0)),
            pl.BlockSpec((None, TAIL, W_C), lambda i: (cur(i) // n_l, 0, 0)),
        ],
        out_shape=[
            jax.ShapeDtypeStruct((nb, L, D), F32),
            jax.ShapeDtypeStruct((nb, HEADS_B // 2, 128, 128), F32),
            jax.ShapeDtypeStruct((nb, TAIL, W_C), F32),
        ],
        scratch_shapes=[
            pltpu.VMEM((TL, N_IN), F32),
            pltpu.VMEM((TAIL + TL, W_C), F32),
            pltpu.VMEM((7, TAIL + TL - 8, W_C), F32),
            pltpu.VMEM((TL, D), BF16),
            pltpu.VMEM((2, TL, D), F32),
        ],
        compiler_params=pltpu.CompilerParams(
            dimension_semantics=("arbitrary",), vmem_limit_bytes=VMEM_LIMIT),
        name=f"layer_prompt_l{layer}",
    )(x, modp, modp, modp, modp, modp, modp, row(gpre), row(gpost), row(gpre2), row(gpost2),
      win_b, wout_b, wup_b, wdn_b, row(alg), row(alb), wcat, abias,
      blb, row(gng), cw, row(cb), row(clg), row(clb), masks, lmask, bdm, gavg)


def _mixs_body(layer, nt,
               x_ref, sh_ref, sc_ref, gt_ref, gpre_ref, gpost_ref, win_ref, wout_ref,
               alg_ref, alb_ref, acoef_ref, abias_ref, blb_ref, gng_ref,
               cw_ref, cb_ref, clg_ref, clb_ref, gavg_ref, cst_ref, s_ref,
               o_ref, so_ref, xg_ref, v_ref,
               z_ref, y_ref, ft_ref, kt_ref, qt_ref, vt_ref, ot_ref):
    hd = pl.program_id(0)
    ns = x_ref.shape[1]

    @pl.when(hd == 0)
    def _():
        x = x_ref[...]
        h = _rms(x, gpre_ref[...]) * (1.0 + sc_ref[...]) + sh_ref[...]
        z_ref[...] = _dot(h.reshape(nt * ns, D).astype(BF16), win_ref[...])

        za = _gelu(z_ref[:, OFF_A:OFF_Q])
        v = _ln(za[:, W_A:], alg_ref[...], alb_ref[...])
        for t in range(nt):
            mixed = abias_ref[pl.ds(t, 1), :]
            for s in range(t + 1):
                mixed = mixed + acoef_ref[pl.ds(t * nt + s, 1), :] * v[s * ns:(s + 1) * ns, :]
            y_ref[t * ns:(t + 1) * ns, 0:W_A] = (za[t * ns:(t + 1) * ns, :W_A] * mixed).astype(BF16)
            v_ref[t] = v[t * ns:(t + 1) * ns, :]

        xg = z_ref[:, OFF_C:OFF_C + W_C] * jax.nn.sigmoid(z_ref[:, OFF_C + W_C:N_IN])
        hist = TAPS - 1
        for t in range(nt):
            xg_ref[t] = xg[t * ns:(t + 1) * ns, :]
        for t in range(nt):
            acc = jnp.zeros((ns, W_C), F32)
            for tap in range(TAPS):
                pos = t + tap
                src = cst_ref[pos] if pos < hist else xg[(pos - hist) * ns:(pos - hist + 1) * ns, :]
                acc = acc + src * cw_ref[pl.ds(tap, 1), :]
            yc = _silu(_ln(acc + cb_ref[...], clg_ref[...], clb_ref[...]))
            y_ref[t * ns:(t + 1) * ns, W_A + W_B:D] = yc.astype(BF16)

        lb = _lower_bound(blb_ref[...], layer)
        for t in range(nt):
            r = slice(t * ns, (t + 1) * ns)
            f = lb + (1.0 - lb) * jax.nn.sigmoid(z_ref[r, OFF_F:OFF_I])
            ft_ref[t] = f.T
            kt_ref[t] = (1.0 - f).T
            qt_ref[t] = (_silu(z_ref[r, OFF_Q:OFF_F]) * (DK ** -0.5)).T
            vt_ref[t] = z_ref[r, OFF_I:OFF_G].T

    base = pl.multiple_of(hd * DK, DK)
    vts = [vt_ref[t, pl.ds(base, DV), :] for t in range(nt)]

    def krow(i, accs):
        srow = s_ref[i]
        new = []
        for t in range(nt):
            srow = ft_ref[t, pl.ds(base + i, 1), :] * srow + kt_ref[t, pl.ds(base + i, 1), :] * vts[t]
            new.append(accs[t] + qt_ref[t, pl.ds(base + i, 1), :] * srow)
        so_ref[i] = srow
        return tuple(new)

    accs = lax.fori_loop(0, DK, krow, tuple(jnp.zeros((DV, ns), F32) for _ in range(nt)))
    for t in range(nt):
        ot_ref[t, pl.ds(base, DV), :] = accs[t]

    @pl.when(hd == HEADS_B - 1)
    def _():
        gavg = gavg_ref[...]
        for t in range(nt):
            r = slice(t * ns, (t + 1) * ns)
            o = ot_ref[t].T
            yb = _group_rms(o, gavg, gng_ref[...]) * _silu(z_ref[r, OFF_G:OFF_C])
            y_ref[r, W_A:W_A + W_B] = yb.astype(BF16)
        y = _dot(y_ref[...], wout_ref[...]).reshape(nt, ns, D)
        o_ref[...] = x_ref[...] + gt_ref[...] * _rms(y, gpost_ref[...])


def _mixs_call(layer, xs, mods, gpre, gpost, win_b, wout_b, alg, alb, acoef, abias, blb, gng,
               cw, cb, clg, clb, gavg, cst, s_in):
    nt, ns, _ = xs.shape
    row = lambda a: a.reshape(1, -1)
    mspec = lambda kk: pl.BlockSpec((ns, D), lambda h: (0, kk))
    hist = TAPS - 1
    return pl.pallas_call(
        functools.partial(_mixs_body, layer, nt),
        grid=(HEADS_B,),
        in_specs=[
            _full((nt, ns, D)),
            mspec(0), mspec(1), mspec(2),
            _full((1, D)), _full((1, D)), _full((D, N_IN)), _full((D, D)),
            _full((1, W_A)), _full((1, W_A)), _full(acoef.shape), _full(abias.shape),
            _full(blb.shape), _full((1, W_B)),
            _full((TAPS, W_C)), _full((1, W_C)), _full((1, W_C)), _full((1, W_C)),
            _full(gavg.shape), _full((hist, ns, W_C)),
            pl.BlockSpec((None, DK, DV, ns), lambda h: (h, 0, 0, 0)),
        ],
        out_specs=[
            _full((nt, ns, D)),
            pl.BlockSpec((None, DK, DV, ns), lambda h: (h, 0, 0, 0)),
            _full((nt, ns, W_C)),
            _full((nt, ns, W_A)),
        ],
        out_shape=[
            jax.ShapeDtypeStruct((nt, ns, D), F32),
            jax.ShapeDtypeStruct((HEADS_B, DK, DV, ns), F32),
            jax.ShapeDtypeStruct((nt, ns, W_C), F32),
            jax.ShapeDtypeStruct((nt, ns, W_A), F32),
        ],
        scratch_shapes=[
            pltpu.VMEM((nt * ns, N_IN), F32),
            pltpu.VMEM((nt * ns, D), BF16),
            pltpu.VMEM((nt, W_B, ns), F32),
            pltpu.VMEM((nt, W_B, ns), F32),
            pltpu.VMEM((nt, W_B, ns), F32),
            pltpu.VMEM((nt, W_B, ns), F32),
            pltpu.VMEM((nt, W_B, ns), F32),
        ],
        compiler_params=pltpu.CompilerParams(
            dimension_semantics=("arbitrary",), vmem_limit_bytes=VMEM_LIMIT),
        name=f"mix_sample_l{layer}",
    )(xs, mods, mods, mods, row(gpre), row(gpost), win_b, wout_b, row(alg), row(alb), acoef, abias,
      blb, row(gng), cw, row(cb), row(clg), row(clb), gavg, cst, s_in)


def _mlp_body(x_ref, sh_ref, sc_ref, gt_ref, gpre_ref, gpost_ref, wup_ref, wdn_ref, o_ref):
    x = x_ref[...]
    a, b, _ = x.shape
    h = (_rms(x, gpre_ref[...]) * (1.0 + sc_ref[...]) + sh_ref[...]).reshape(a * b, D).astype(BF16)
    cw = 1024
    y = jnp.zeros((a * b, D), F32)
    for c in range(D_FF // cw):
        u = jnp.maximum(_dot(h, wup_ref[:, c * cw:(c + 1) * cw]), 0.0)
        y = y + _dot((u * u).astype(BF16), wdn_ref[c * cw:(c + 1) * cw, :])
    o_ref[...] = x + gt_ref[...] * _rms(y, gpost_ref[...]).reshape(a, b, D)


def _mlp_call(name, x, mod, gpre, gpost, wup_b, wdn_b, per_seq):
    row = lambda v: v.reshape(1, -1)
    if per_seq:
        n, L, _ = x.shape
        grid = (n, L // TM)
        xspec = pl.BlockSpec((1, TM, D), lambda i, j: (i, j, 0))
        mspec = lambda kk: pl.BlockSpec((1, 1, D), lambda i, j: (i, 0, kk))
    else:
        t, n, _ = x.shape
        grid = (1, 1)
        xspec = pl.BlockSpec((t, n, D), lambda i, j: (0, 0, 0))
        mspec = lambda kk: pl.BlockSpec((1, n, D), lambda i, j: (0, 0, kk))
    return pl.pallas_call(
        _mlp_body,
        grid=grid,
        in_specs=[xspec, mspec(3), mspec(4), mspec(5),
                  pl.BlockSpec((1, D), lambda i, j: (0, 0)), pl.BlockSpec((1, D), lambda i, j: (0, 0)),
                  pl.BlockSpec((D, D_FF), lambda i, j: (0, 0)), pl.BlockSpec((D_FF, D), lambda i, j: (0, 0))],
        out_specs=xspec,
        out_shape=jax.ShapeDtypeStruct(x.shape, F32),
        compiler_params=pltpu.CompilerParams(
            dimension_semantics=("arbitrary", "arbitrary"), vmem_limit_bytes=VMEM_LIMIT),
        name=name,
    )(x, mod, mod, mod, row(gpre), row(gpost), wup_b, wdn_b)


def kernel(x_prompt, x_sample, state_hgrn, state_conv, c_prompt, c_sample, w_ada, b_ada, g_pre_mix, g_post_mix, g_pre_mlp, g_post_mlp, w_in, a_ln_g, a_ln_b, a_w_s, a_b_s, b_lb, b_gn_g, c_w_dw, c_b_dw, c_ln_g, c_ln_b, w_out, w_up, w_down):
    depth = w_in.shape[0]
    nb = x_prompt.shape[0]
    ns, nt, _ = x_sample.shape
    hist = TAPS - 1

    mod = _mod_call(jnp.concatenate([c_sample, c_prompt], axis=0), w_ada, b_ada)
    masks = jnp.asarray(_level_masks(), dtype=BF16)
    lmask = jnp.asarray(_first_head_lanes(), dtype=BF16)
    bdm = jnp.asarray(_block_diag_mask())
    gavg = jnp.asarray(_group_avg(), dtype=BF16)

    xp = x_prompt
    xs = jnp.transpose(x_sample, (1, 0, 2))
    hg_p, hg_s, cv_p, cv_s, v_s = [], [], [], [], []
    for l in range(depth):
        win_b, wout_b = w_in[l].astype(BF16), w_out[l].astype(BF16)
        wup_b, wdn_b = w_up[l].astype(BF16), w_down[l].astype(BF16)
        mods = mod[l, :ns]
        modp = mod[l, ns:].reshape(nb, 1, 6 * D)
        wcat = jnp.transpose(a_w_s[l], (1, 0, 2)).reshape(CHUNK_A, HEADS_A * CHUNK_A)
        abias = jnp.repeat(a_b_s[l].T, DH_A, axis=1)
        acoef = jnp.repeat(jnp.transpose(a_w_s[l][:, :nt, :nt], (1, 2, 0)).reshape(nt * nt, HEADS_A),
                           DH_A, axis=1)

        xp, st_p, tail_p = _layerp_call(
            l, xp, modp, g_pre_mix[l], g_post_mix[l], g_pre_mlp[l], g_post_mlp[l], win_b, wout_b, wup_b, wdn_b,
            a_ln_g[l], a_ln_b[l], wcat, abias, b_lb, b_gn_g[l], c_w_dw[l], c_b_dw[l], c_ln_g[l], c_ln_b[l],
            masks, lmask, bdm, gavg)

        cst = jnp.transpose(state_conv[l], (1, 0, 2))
        s_in = jnp.transpose(state_hgrn[l], (1, 2, 3, 0))
        xs, s_out, xg_s, vrow = _mixs_call(
            l, xs, mods, g_pre_mix[l], g_post_mix[l], win_b, wout_b, a_ln_g[l], a_ln_b[l], acoef,
            abias[:nt], b_lb, b_gn_g[l], c_w_dw[l], c_b_dw[l], c_ln_g[l], c_ln_b[l], gavg, cst, s_in)
        xs = _mlp_call(f"mlp_sample_l{l}", xs, mods.reshape(1, ns, 6 * D), g_pre_mlp[l], g_post_mlp[l],
                       wup_b, wdn_b, False)

        st6 = st_p.reshape(nb, HEADS_B // 2, 2, DV, 2, DK)
        hg = jnp.stack([st6[:, :, 0, :, 0, :], st6[:, :, 1, :, 1, :]], axis=2)
        hg_p.append(jnp.transpose(hg.reshape(nb, HEADS_B, DV, DK), (0, 1, 3, 2)))
        hg_s.append(jnp.transpose(s_out, (3, 0, 1, 2)))
        cv_p.append(tail_p[:, TAIL - hist:, :])
        cv_s.append(jnp.concatenate([state_conv[l][:, nt:, :], jnp.transpose(xg_s, (1, 0, 2))], axis=1))
        v_s.append(jnp.transpose(vrow, (1, 0, 2)))

    return (xp, jnp.transpose(xs, (1, 0, 2)), jnp.stack(hg_p), jnp.stack(hg_s),
            jnp.stack(cv_p), jnp.stack(cv_s), jnp.stack(v_s))
```

```python
import functools

import numpy as np
import jax
import jax.numpy as jnp
from jax import lax
from jax.experimental import pallas as pl
from jax.experimental.pallas import tpu as pltpu

F32 = jnp.float32
BF16 = jnp.bfloat16

D = 1024
HEADS_A, DH_A, W_A, CHUNK_A = 4, 64, 256, 128
HEADS_B, DK, DV, W_B = 8, 64, 64, 512
SUB = 16
TILE_B = 128
W_C, TAPS = 256, 31
TAIL = 32
D_FF = 4096
N_IN = 3072
EPS = 1e-6
OFF_A, OFF_Q, OFF_F, OFF_I, OFF_G, OFF_C = 0, 512, 1024, 1536, 2048, 2560

TL = 256
TM = 256
VMEM_LIMIT = 56 * 1024 * 1024


def _dot(a, b):
    return jnp.dot(a, b, preferred_element_type=F32)


def _dot_nt(a, b):
    return lax.dot_general(a, b, (((1,), (1,)), ((), ())), preferred_element_type=F32)


def _dot_tn(a, b):
    return lax.dot_general(a, b, (((0,), (0,)), ((), ())), preferred_element_type=F32)


def _rms(x, g):
    return x * lax.rsqrt(jnp.mean(x * x, axis=-1, keepdims=True) + EPS) * g


def _ln(x, g, b):
    xc = x - jnp.mean(x, axis=-1, keepdims=True)
    return xc * lax.rsqrt(jnp.mean(xc * xc, axis=-1, keepdims=True) + EPS) * g + b


def _silu(x):
    return x * jax.nn.sigmoid(x)


def _gelu(x):
    return 0.5 * x * (1.0 + lax.erf(x * 0.7071067811865476))


def _lower_bound(blb, layer):
    m = jnp.max(blb, axis=0, keepdims=True)
    e = jnp.exp(blb - m)
    tot = jnp.sum(e, axis=0, keepdims=True)
    acc = jnp.zeros_like(tot)
    for i in range(1, layer + 1):
        acc = acc + e[i:i + 1, :]
    return acc / tot


def _group_rms(o, gavg, g):
    o2 = o * o
    hi = o2.astype(BF16)
    lo = (o2 - hi.astype(F32)).astype(BF16)
    ms = _dot(hi, gavg) + _dot(lo, gavg)
    return o * lax.rsqrt(ms + EPS) * g


def _mod_body(c_ref, w_ref, b_ref, o_ref):
    a = _silu(c_ref[...]).astype(BF16)
    o_ref[...] = _dot(a, w_ref[...].astype(BF16)) + b_ref[...]


def _mod_call(c_all, w_ada, b_ada):
    depth, _, n6 = w_ada.shape
    rows = c_all.shape[0]
    bn = 1536
    return pl.pallas_call(
        _mod_body,
        grid=(depth, n6 // bn),
        in_specs=[
            pl.BlockSpec((rows, D), lambda l, j: (0, 0)),
            pl.BlockSpec((None, D, bn), lambda l, j: (l, 0, j)),
            pl.BlockSpec((None, 1, bn), lambda l, j: (l, 0, j)),
        ],
        out_specs=pl.BlockSpec((None, rows, bn), lambda l, j: (l, 0, j)),
        out_shape=jax.ShapeDtypeStruct((depth, rows, n6), F32),
        compiler_params=pltpu.CompilerParams(
            dimension_semantics=("arbitrary", "arbitrary"), vmem_limit_bytes=VMEM_LIMIT),
        name="adaln_mod",
    )(c_all, w_ada, b_ada.reshape(depth, 1, n6))


def _gmlp_chunk(za, ln_g, ln_b, wcat_b, bias):
    z = _gelu(za)
    u = z[:, :W_A]
    v = _ln(z[:, W_A:], ln_g, ln_b)
    lane_h = lax.broadcasted_iota(jnp.int32, (CHUNK_A, W_A), 1) // DH_A
    vbd = jnp.concatenate(
        [jnp.where(lane_h == h, v, 0.0).astype(BF16) for h in range(HEADS_A)], axis=0)
    mixed = _dot(wcat_b, vbd) + bias
    return u * mixed


def _hgrn_tile(zq, zf, zi, zg, lb, gn_g, st_ref, masks_ref, lmask_ref, bd_ref, gavg, keep):
    nch = TILE_B // SUB
    q = _silu(zq) * (DK ** -0.5)
    f = lb + (1.0 - lb) * jax.nn.sigmoid(zf)
    lf = jnp.log(f)
    k = 1.0 - f

    r15 = lax.broadcasted_iota(jnp.int32, (TILE_B, W_B), 0) & (SUB - 1)
    cs = lf
    for d in (1, 2, 4, 8):
        cs = cs + jnp.where(r15 >= d, pltpu.roll(cs, d, 0), 0.0)
    tot = [cs[SUB * c + SUB - 1:SUB * c + SUB, :] for c in range(nch)]
    tot_b = jnp.concatenate([jnp.broadcast_to(t, (SUB, W_B)) for t in tot], axis=0)
    q_in = q * jnp.exp(cs)
    k_out = k * jnp.exp(tot_b - cs)

    q0, k0 = [], []
    for c in range(nch):
        r = slice(SUB * c, SUB * c + SUB)
        if c % 2 == 0:
            q0.append(q[r] * jnp.exp(cs[r] - tot[c]))
            k0.append(k_out[r])
        else:
            q0.append(q_in[r])
            k0.append(k[r] * jnp.exp(-cs[r]))
    q0 = jnp.concatenate(q0, axis=0)
    k0 = jnp.concatenate(k0, axis=0)

    pre = [jnp.zeros((1, W_B), F32)]
    for c in range(nch):
        pre.append(pre[-1] + tot[c])

    def rows(fn):
        parts = []
        for c in range(nch):
            e = fn(c)
            parts.append(jnp.zeros((SUB, W_B), F32) if e is None
                         else jnp.broadcast_to(jnp.exp(e), (SUB, W_B)))
        return jnp.concatenate(parts, axis=0)

    a_st = rows(lambda c: pre[c])
    b_st = rows(lambda c: pre[nch] - pre[c + 1])
    a_64 = rows(lambda c: pre[c] - pre[4] if c >= 4 else None)
    b_64 = rows(lambda c: pre[4] - pre[c + 1] if c < 4 else None)
    a_32 = rows(lambda c: pre[c] - pre[c & ~3 | 2] if (c & 2) else None)
    b_32 = rows(lambda c: pre[c & ~3 | 2] - pre[c + 1] if not (c & 2) else None)
    decay = jnp.exp(pre[nch])

    m0 = lmask_ref[...] > 0.5
    zb = jnp.zeros((TILE_B, 128), BF16)

    def stack2(x):
        xb = x.astype(BF16)
        return jnp.concatenate([jnp.where(m0, xb, zb), jnp.where(m0, zb, xb)], axis=0)

    levels = ((q0, k0), (q_in * a_32, k_out * b_32), (q_in * a_64, k_out * b_64))
    q_st = q_in * a_st
    k_st = k_out * b_st
    v_t = zi.T
    bd = jnp.logical_and(bd_ref[...] > 0.5, keep)
    decay = jnp.where(keep, decay, 1.0)
    mk = [masks_ref[lv] > 0.5 for lv in range(len(levels))]

    outs = []
    for p in range(HEADS_B // 2):
        sl = slice(128 * p, 128 * p + 128)
        att = jnp.zeros((TILE_B, 2 * TILE_B), BF16)
        for lv, (qa, ka) in reversed(list(enumerate(levels))):
            s = _dot_nt(qa[:, sl].astype(BF16), stack2(ka[:, sl]))
            att = jnp.where(mk[lv], s.astype(BF16), att)
        o = _dot(att, stack2(zi[:, sl]))
        st = st_ref[p]
        o = o + _dot_nt(q_st[:, sl].astype(BF16), st.astype(BF16))
        upd = _dot(v_t[sl, :].astype(BF16), k_st[:, sl].astype(BF16))
        st_ref[p] = st * decay[:, sl] + jnp.where(bd, upd, 0.0)
        outs.append(o)
    o = jnp.concatenate(outs, axis=1)
    return _group_rms(o, gavg, gn_g) * _silu(zg)


def _layerp_body(layer, n_l, n_tiles,
                 x_ref, xn_ref, sh_ref, sc_ref, gt_ref, shn_ref, scn_ref, sh2_ref, sc2_ref, gt2_ref,
                 gpre_ref, gpost_ref, gpre2_ref, gpost2_ref, win_ref, wout_ref, wup_ref, wdn_ref,
                 alg_ref, alb_ref, wcat_ref, abias_ref, blb_ref, gng_ref,
                 cw_ref, cb_ref, clg_ref, clb_ref, masks_ref, lmask_ref, bd_ref, gavg_ref,
                 o_ref, st_ref, tail_ref,
                 z_ref, xx_ref, xs_ref, y_ref, x1_ref, hm_ref, h_ref):
    i = pl.program_id(0)
    keep = i < n_tiles

    @pl.when(jnp.logical_and(i % n_l == 0, keep))
    def _():
        st_ref[...] = jnp.zeros_like(st_ref)
        tail_ref[...] = jnp.zeros_like(tail_ref)

    @pl.when(i == 0)
    def _():
        x1_ref[1] = jnp.zeros((TL, D), F32)
        hm_ref[...] = jnp.zeros((TL, D), BF16)
        h_ref[...] = (_rms(x_ref[...], gpre_ref[...] * (1.0 + sc_ref[...])) + sh_ref[...]).astype(BF16)

    cwid = D_FF // 4

    def mlp_chunk(c, acc):
        u = jnp.maximum(_dot(hm_ref[...], wup_ref[:, c * cwid:(c + 1) * cwid]), 0.0)
        part = _dot((u * u).astype(BF16), wdn_ref[c * cwid:(c + 1) * cwid, :])
        return part if acc is None else acc + part

    z_ref[:, OFF_C:N_IN] = _dot(h_ref[...], win_ref[:, OFF_C:N_IN])
    z_ref[:, 0:OFF_C] = _dot(h_ref[...], win_ref[:, 0:OFF_C])

    ym = mlp_chunk(0, None)

    xg = z_ref[:, OFF_C:OFF_C + W_C] * jax.nn.sigmoid(z_ref[:, OFF_C + W_C:N_IN])
    old_tail = tail_ref[...]
    xx_ref[0:TAIL, :] = old_tail
    xx_ref[TAIL:TAIL + TL, :] = xg
    tail_ref[...] = jnp.where(keep, xg[TL - TAIL:TL, :], old_tail)
    span = TL + TAIL - 8
    for r in range(1, 8):
        xs_ref[r - 1] = xx_ref[pl.ds(r, span), :]
    rb = 64
    for c in range(TL // rb):
        acc = jnp.zeros((rb, W_C), F32)
        for t in range(TAPS):
            a, r = divmod(TAIL - (TAPS - 1) + t, 8)
            src = xx_ref[pl.ds(c * rb + 8 * a, rb), :] if r == 0 else xs_ref[r - 1, pl.ds(c * rb + 8 * a, rb), :]
            acc = acc + src * cw_ref[pl.ds(t, 1), :]
        yc = _silu(_ln(acc + cb_ref[...], clg_ref[...], clb_ref[...]))
        y_ref[c * rb:(c + 1) * rb, W_A + W_B:D] = yc.astype(BF16)

    lb = _lower_bound(blb_ref[...], layer)
    gavg = gavg_ref[...]
    for t in range(TL // TILE_B):
        ym = mlp_chunk(1 + t, ym)
        r = slice(t * TILE_B, (t + 1) * TILE_B)
        yb = _hgrn_tile(z_ref[r, OFF_Q:OFF_F], z_ref[r, OFF_F:OFF_I], z_ref[r, OFF_I:OFF_G],
                        z_ref[r, OFF_G:OFF_C], lb, gng_ref[...], st_ref, masks_ref, lmask_ref, bd_ref, gavg,
                        keep)
        y_ref[r, W_A:W_A + W_B] = yb.astype(BF16)

    tri = (lax.broadcasted_iota(jnp.int32, (CHUNK_A, HEADS_A * CHUNK_A), 1) & (CHUNK_A - 1)
           ) <= lax.broadcasted_iota(jnp.int32, (CHUNK_A, HEADS_A * CHUNK_A), 0)
    wcat_b = jnp.where(tri, wcat_ref[...], 0.0).astype(BF16)
    for c in range(TL // CHUNK_A):
        r = slice(c * CHUNK_A, (c + 1) * CHUNK_A)
        ya = _gmlp_chunk(z_ref[r, OFF_A:OFF_Q], alg_ref[...], alb_ref[...], wcat_b, abias_ref[...])
        y_ref[r, 0:W_A] = ya.astype(BF16)

    ym = mlp_chunk(3, ym)

    o_ref[...] = x1_ref[(i + 1) % 2] + _rms(ym, gt2_ref[...] * gpost2_ref[...])
    y = _dot(y_ref[...], wout_ref[...])
    x1 = x_ref[...] + _rms(y, gt_ref[...] * gpost_ref[...])
    x1_ref[i % 2] = x1
    hm_ref[...] = (_rms(x1, gpre2_ref[...] * (1.0 + sc2_ref[...])) + sh2_ref[...]).astype(BF16)
    h_ref[...] = (_rms(xn_ref[...], gpre_ref[...] * (1.0 + scn_ref[...])) + shn_ref[...]).astype(BF16)


def _level_masks():
    t = np.arange(TILE_B)[:, None]
    s = np.arange(2 * TILE_B)[None, :] % TILE_B
    ct, cs = t // SUB, s // SUB
    m = [
        (ct // 2 == cs // 2) & (s <= t),
        (ct // 2 == cs // 2 + 1) & ((ct // 2) % 2 == 1),
        (ct // 4 == cs // 4 + 1) & ((ct // 4) % 2 == 1),
    ]
    return np.stack(m).astype(np.float32)


def _first_head_lanes():
    return np.broadcast_to(np.arange(128)[None, :] < DK, (TILE_B, 128)).astype(np.float32)


def _block_diag_mask():
    i = np.arange(128)
    return ((i[:, None] // DK) == (i[None, :] // DK)).astype(np.float32)


def _group_avg():
    i = np.arange(W_B)
    return ((i[:, None] // DV) == (i[None, :] // DV)).astype(np.float32) / DV


def _full(shape):
    nd = len(shape)
    return pl.BlockSpec(shape, lambda *_: (0,) * nd)


def _layerp_call(layer, x, modp, gpre, gpost, gpre2, gpost2, win_b, wout_b, wup_b, wdn_b,
                 alg, alb, wcat, abias, blb, gng, cw, cb, clg, clb, masks, lmask, bdm, gavg):
    nb, L, _ = x.shape
    n_l = L // TL
    n_tiles = nb * n_l
    row = lambda a: a.reshape(1, -1)
    cur = lambda i: jnp.minimum(i, n_tiles - 1)
    prv = lambda i: jnp.maximum(i - 1, 0)
    mcur = lambda kk: pl.BlockSpec((None, 1, D), lambda i: (cur(i) // n_l, 0, kk))
    mprv = lambda kk: pl.BlockSpec((None, 1, D), lambda i: (prv(i) // n_l, 0, kk))
    nxt = lambda i: jnp.minimum(i + 1, n_tiles - 1)
    mnxt = lambda kk: pl.BlockSpec((None, 1, D), lambda i: (nxt(i) // n_l, 0, kk))
    once = lambda shape: pl.BlockSpec(shape, lambda i: (0,) * len(shape), pipeline_mode=pl.Buffered(1))
    return pl.pallas_call(
        functools.partial(_layerp_body, layer, n_l, n_tiles),
        grid=(n_tiles + 1,),
        in_specs=[
            pl.BlockSpec((None, TL, D), lambda i: (cur(i) // n_l, cur(i) % n_l, 0)),
            pl.BlockSpec((None, TL, D), lambda i: (nxt(i) // n_l, nxt(i) % n_l, 0)),
            mcur(0), mcur(1), mcur(2), mnxt(0), mnxt(1), mcur(3), mcur(4), mprv(5),
            _full((1, D)), _full((1, D)), _full((1, D)), _full((1, D)),
            once((D, N_IN)), once((D, D)), once((D, D_FF)), once((D_FF, D)),
            _full((1, W_A)), _full((1, W_A)), _full((CHUNK_A, HEADS_A * CHUNK_A)), _full((CHUNK_A, W_A)),
            _full(blb.shape), _full((1, W_B)),
            _full((TAPS, W_C)), _full((1, W_C)), _full((1, W_C)), _full((1, W_C)),
            _full(masks.shape), _full(lmask.shape), _full(bdm.shape), _full(gavg.shape),
        ],
        out_specs=[
            pl.BlockSpec((None, TL, D), lambda i: (prv(i) // n_l, prv(i) % n_l, 0)),
            pl.BlockSpec((None, HEADS_B // 2, 128, 128), lambda i: (cur(i) // n_l, 0, 0, 0)),
            pl.BlockSpec((None, TAIL, W_C), lambda i: (cur(i) // n_l, 0, 0)),
        ],
        out_shape=[
            jax.ShapeDtypeStruct((nb, L, D), F32),
            jax.ShapeDtypeStruct((nb, HEADS_B // 2, 128, 128), F32),
            jax.ShapeDtypeStruct((nb, TAIL, W_C), F32),
        ],
        scratch_shapes=[
            pltpu.VMEM((TL, N_IN), F32),
            pltpu.VMEM((TAIL + TL, W_C), F32),
            pltpu.VMEM((7, TAIL + TL - 8, W_C), F32),
            pltpu.VMEM((TL, D), BF16),
            pltpu.VMEM((2, TL, D), F32),
            pltpu.VMEM((TL, D), BF16),
            pltpu.VMEM((TL, D), BF16),
        ],
        compiler_params=pltpu.CompilerParams(
            dimension_semantics=("arbitrary",), vmem_limit_bytes=VMEM_LIMIT),
        name=f"layer_prompt_l{layer}",
    )(x, x, modp, modp, modp, modp, modp, modp, modp, modp, row(gpre), row(gpost), row(gpre2), row(gpost2),
      win_b, wout_b, wup_b, wdn_b, row(alg), row(alb), wcat, abias,
      blb, row(gng), cw, row(cb), row(clg), row(clb), masks, lmask, bdm, gavg)


def _mixs_body(layer, nt,
               x_ref, sh_ref, sc_ref, gt_ref, gpre_ref, gpost_ref, win_ref, wout_ref,
               alg_ref, alb_ref, acoef_ref, abias_ref, blb_ref, gng_ref,
               cw_ref, cb_ref, clg_ref, clb_ref, gavg_ref, cst_ref, s_ref,
               o_ref, so_ref, xg_ref, v_ref,
               z_ref, y_ref, ft_ref, kt_ref, qt_ref, vt_ref, ot_ref):
    hd = pl.program_id(0)
    ns = x_ref.shape[1]

    @pl.when(hd == 0)
    def _():
        x = x_ref[...]
        h = _rms(x, gpre_ref[...]) * (1.0 + sc_ref[...]) + sh_ref[...]
        z_ref[...] = _dot(h.reshape(nt * ns, D).astype(BF16), win_ref[...])

        za = _gelu(z_ref[:, OFF_A:OFF_Q])
        v = _ln(za[:, W_A:], alg_ref[...], alb_ref[...])
        for t in range(nt):
            mixed = abias_ref[pl.ds(t, 1), :]
            for s in range(t + 1):
                mixed = mixed + acoef_ref[pl.ds(t * nt + s, 1), :] * v[s * ns:(s + 1) * ns, :]
            y_ref[t * ns:(t + 1) * ns, 0:W_A] = (za[t * ns:(t + 1) * ns, :W_A] * mixed).astype(BF16)
            v_ref[t] = v[t * ns:(t + 1) * ns, :]

        xg = z_ref[:, OFF_C:OFF_C + W_C] * jax.nn.sigmoid(z_ref[:, OFF_C + W_C:N_IN])
        hist = TAPS - 1
        for t in range(nt):
            xg_ref[t] = xg[t * ns:(t + 1) * ns, :]
        for t in range(nt):
            acc = jnp.zeros((ns, W_C), F32)
            for tap in range(TAPS):
                pos = t + tap
                src = cst_ref[pos] if pos < hist else xg[(pos - hist) * ns:(pos - hist + 1) * ns, :]
                acc = acc + src * cw_ref[pl.ds(tap, 1), :]
            yc = _silu(_ln(acc + cb_ref[...], clg_ref[...], clb_ref[...]))
            y_ref[t * ns:(t + 1) * ns, W_A + W_B:D] = yc.astype(BF16)

        lb = _lower_bound(blb_ref[...], layer)
        for t in range(nt):
            r = slice(t * ns, (t + 1) * ns)
            f = lb + (1.0 - lb) * jax.nn.sigmoid(z_ref[r, OFF_F:OFF_I])
            ft_ref[t] = f.T
            kt_ref[t] = (1.0 - f).T
            qt_ref[t] = (_silu(z_ref[r, OFF_Q:OFF_F]) * (DK ** -0.5)).T
            vt_ref[t] = z_ref[r, OFF_I:OFF_G].T

    base = pl.multiple_of(hd * DK, DK)
    vts = [vt_ref[t, pl.ds(base, DV), :] for t in range(nt)]

    def krow(i, accs):
        srow = s_ref[i]
        new = []
        for t in range(nt):
            srow = ft_ref[t, pl.ds(base + i, 1), :] * srow + kt_ref[t, pl.ds(base + i, 1), :] * vts[t]
            new.append(accs[t] + qt_ref[t, pl.ds(base + i, 1), :] * srow)
        so_ref[i] = srow
        return tuple(new)

    accs = lax.fori_loop(0, DK, krow, tuple(jnp.zeros((DV, ns), F32) for _ in range(nt)))
    for t in range(nt):
        ot_ref[t, pl.ds(base, DV), :] = accs[t]

    @pl.when(hd == HEADS_B - 1)
    def _():
        gavg = gavg_ref[...]
        for t in range(nt):
            r = slice(t * ns, (t + 1) * ns)
            o = ot_ref[t].T
            yb = _group_rms(o, gavg, gng_ref[...]) * _silu(z_ref[r, OFF_G:OFF_C])
            y_ref[r, W_A:W_A + W_B] = yb.astype(BF16)
        y = _dot(y_ref[...], wout_ref[...]).reshape(nt, ns, D)
        o_ref[...] = x_ref[...] + gt_ref[...] * _rms(y, gpost_ref[...])


def _mixs_call(layer, xs, mods, gpre, gpost, win_b, wout_b, alg, alb, acoef, abias, blb, gng,
               cw, cb, clg, clb, gavg, cst, s_in):
    nt, ns, _ = xs.shape
    row = lambda a: a.reshape(1, -1)
    mspec = lambda kk: pl.BlockSpec((ns, D), lambda h: (0, kk))
    hist = TAPS - 1
    return pl.pallas_call(
        functools.partial(_mixs_body, layer, nt),
        grid=(HEADS_B,),
        in_specs=[
            _full((nt, ns, D)),
            mspec(0), mspec(1), mspec(2),
            _full((1, D)), _full((1, D)), _full((D, N_IN)), _full((D, D)),
            _full((1, W_A)), _full((1, W_A)), _full(acoef.shape), _full(abias.shape),
            _full(blb.shape), _full((1, W_B)),
            _full((TAPS, W_C)), _full((1, W_C)), _full((1, W_C)), _full((1, W_C)),
            _full(gavg.shape), _full((hist, ns, W_C)),
            pl.BlockSpec((None, DK, DV, ns), lambda h: (h, 0, 0, 0)),
        ],
        out_specs=[
            _full((nt, ns, D)),
            pl.BlockSpec((None, DK, DV, ns), lambda h: (h, 0, 0, 0)),
            _full((nt, ns, W_C)),
            _full((nt, ns, W_A)),
        ],
        out_shape=[
            jax.ShapeDtypeStruct((nt, ns, D), F32),
            jax.ShapeDtypeStruct((HEADS_B, DK, DV, ns), F32),
            jax.ShapeDtypeStruct((nt, ns, W_C), F32),
            jax.ShapeDtypeStruct((nt, ns, W_A), F32),
        ],
        scratch_shapes=[
            pltpu.VMEM((nt * ns, N_IN), F32),
            pltpu.VMEM((nt * ns, D), BF16),
            pltpu.VMEM((nt, W_B, ns), F32),
            pltpu.VMEM((nt, W_B, ns), F32),
            pltpu.VMEM((nt, W_B, ns), F32),
            pltpu.VMEM((nt, W_B, ns), F32),
            pltpu.VMEM((nt, W_B, ns), F32),
        ],
        compiler_params=pltpu.CompilerParams(
            dimension_semantics=("arbitrary",), vmem_limit_bytes=VMEM_LIMIT),
        name=f"mix_sample_l{layer}",
    )(xs, mods, mods, mods, row(gpre), row(gpost), win_b, wout_b, row(alg), row(alb), acoef, abias,
      blb, row(gng), cw, row(cb), row(clg), row(clb), gavg, cst, s_in)


def _mlp_body(x_ref, sh_ref, sc_ref, gt_ref, gpre_ref, gpost_ref, wup_ref, wdn_ref, o_ref):
    x = x_ref[...]
    a, b, _ = x.shape
    h = (_rms(x, gpre_ref[...]) * (1.0 + sc_ref[...]) + sh_ref[...]).reshape(a * b, D).astype(BF16)
    cw = 1024
    y = jnp.zeros((a * b, D), F32)
    for c in range(D_FF // cw):
        u = jnp.maximum(_dot(h, wup_ref[:, c * cw:(c + 1) * cw]), 0.0)
        y = y + _dot((u * u).astype(BF16), wdn_ref[c * cw:(c + 1) * cw, :])
    o_ref[...] = x + gt_ref[...] * _rms(y, gpost_ref[...]).reshape(a, b, D)


def _mlp_call(name, x, mod, gpre, gpost, wup_b, wdn_b, per_seq):
    row = lambda v: v.reshape(1, -1)
    if per_seq:
        n, L, _ = x.shape
        grid = (n, L // TM)
        xspec = pl.BlockSpec((1, TM, D), lambda i, j: (i, j, 0))
        mspec = lambda kk: pl.BlockSpec((1, 1, D), lambda i, j: (i, 0, kk))
    else:
        t, n, _ = x.shape
        grid = (1, 1)
        xspec = pl.BlockSpec((t, n, D), lambda i, j: (0, 0, 0))
        mspec = lambda kk: pl.BlockSpec((1, n, D), lambda i, j: (0, 0, kk))
    return pl.pallas_call(
        _mlp_body,
        grid=grid,
        in_specs=[xspec, mspec(3), mspec(4), mspec(5),
                  pl.BlockSpec((1, D), lambda i, j: (0, 0)), pl.BlockSpec((1, D), lambda i, j: (0, 0)),
                  pl.BlockSpec((D, D_FF), lambda i, j: (0, 0)), pl.BlockSpec((D_FF, D), lambda i, j: (0, 0))],
        out_specs=xspec,
        out_shape=jax.ShapeDtypeStruct(x.shape, F32),
        compiler_params=pltpu.CompilerParams(
            dimension_semantics=("arbitrary", "arbitrary"), vmem_limit_bytes=VMEM_LIMIT),
        name=name,
    )(x, mod, mod, mod, row(gpre), row(gpost), wup_b, wdn_b)


def kernel(x_prompt, x_sample, state_hgrn, state_conv, c_prompt, c_sample, w_ada, b_ada, g_pre_mix, g_post_mix, g_pre_mlp, g_post_mlp, w_in, a_ln_g, a_ln_b, a_w_s, a_b_s, b_lb, b_gn_g, c_w_dw, c_b_dw, c_ln_g, c_ln_b, w_out, w_up, w_down):
    depth = w_in.shape[0]
    nb = x_prompt.shape[0]
    ns, nt, _ = x_sample.shape
    hist = TAPS - 1

    mod = _mod_call(jnp.concatenate([c_sample, c_prompt], axis=0), w_ada, b_ada)
    masks = jnp.asarray(_level_masks(), dtype=BF16)
    lmask = jnp.asarray(_first_head_lanes(), dtype=BF16)
    bdm = jnp.asarray(_block_diag_mask())
    gavg = jnp.asarray(_group_avg(), dtype=BF16)

    xp = x_prompt
    xs = jnp.transpose(x_sample, (1, 0, 2))
    hg_p, hg_s, cv_p, cv_s, v_s = [], [], [], [], []
    for l in range(depth):
        win_b, wout_b = w_in[l].astype(BF16), w_out[l].astype(BF16)
        wup_b, wdn_b = w_up[l].astype(BF16), w_down[l].astype(BF16)
        mods = mod[l, :ns]
        modp = mod[l, ns:].reshape(nb, 1, 6 * D)
        wcat = jnp.transpose(a_w_s[l], (1, 0, 2)).reshape(CHUNK_A, HEADS_A * CHUNK_A)
        abias = jnp.repeat(a_b_s[l].T, DH_A, axis=1)
        acoef = jnp.repeat(jnp.transpose(a_w_s[l][:, :nt, :nt], (1, 2, 0)).reshape(nt * nt, HEADS_A),
                           DH_A, axis=1)

        xp, st_p, tail_p = _layerp_call(
            l, xp, modp, g_pre_mix[l], g_post_mix[l], g_pre_mlp[l], g_post_mlp[l], win_b, wout_b, wup_b, wdn_b,
            a_ln_g[l], a_ln_b[l], wcat, abias, b_lb, b_gn_g[l], c_w_dw[l], c_b_dw[l], c_ln_g[l], c_ln_b[l],
            masks, lmask, bdm, gavg)

        cst = jnp.transpose(state_conv[l], (1, 0, 2))
        s_in = jnp.transpose(state_hgrn[l], (1, 2, 3, 0))
        xs, s_out, xg_s, vrow = _mixs_call(
            l, xs, mods, g_pre_mix[l], g_post_mix[l], win_b, wout_b, a_ln_g[l], a_ln_b[l], acoef,
            abias[:nt], b_lb, b_gn_g[l], c_w_dw[l], c_b_dw[l], c_ln_g[l], c_ln_b[l], gavg, cst, s_in)
        xs = _mlp_call(f"mlp_sample_l{l}", xs, mods.reshape(1, ns, 6 * D), g_pre_mlp[l], g_post_mlp[l],
                       wup_b, wdn_b, False)

        st6 = st_p.reshape(nb, HEADS_B // 2, 2, DV, 2, DK)
        hg = jnp.stack([st6[:, :, 0, :, 0, :], st6[:, :, 1, :, 1, :]], axis=2)
        hg_p.append(jnp.transpose(hg.reshape(nb, HEADS_B, DV, DK), (0, 1, 3, 2)))
        hg_s.append(jnp.transpose(s_out, (3, 0, 1, 2)))
        cv_p.append(tail_p[:, TAIL - hist:, :])
        cv_s.append(jnp.concatenate([state_conv[l][:, nt:, :], jnp.transpose(xg_s, (1, 0, 2))], axis=1))
        v_s.append(jnp.transpose(vrow, (1, 0, 2)))

    return (xp, jnp.transpose(xs, (1, 0, 2)), jnp.stack(hg_p), jnp.stack(hg_s),
            jnp.stack(cv_p), jnp.stack(cv_s), jnp.stack(v_s))
```

```python
import functools

import numpy as np
import jax
import jax.numpy as jnp
from jax import lax
from jax.experimental import pallas as pl
from jax.experimental.pallas import tpu as pltpu

F32 = jnp.float32
BF16 = jnp.bfloat16

D = 1024
HEADS_A, DH_A, W_A, CHUNK_A = 4, 64, 256, 128
HEADS_B, DK, DV, W_B = 8, 64, 64, 512
SUB = 16
TILE_B = 128
W_C, TAPS = 256, 31
TAIL = 32
D_FF = 4096
N_IN = 3072
EPS = 1e-6
OFF_A, OFF_Q, OFF_F, OFF_I, OFF_G, OFF_C = 0, 512, 1024, 1536, 2048, 2560

TL = 256
VMEM_LIMIT = 56 * 1024 * 1024


def _dot(a, b):
    return jnp.dot(a, b, preferred_element_type=F32)


def _dot_nt(a, b):
    return lax.dot_general(a, b, (((1,), (1,)), ((), ())), preferred_element_type=F32)


def _rms(x, g):
    return x * lax.rsqrt(jnp.mean(x * x, axis=-1, keepdims=True) + EPS) * g


def _ln(x, g, b):
    xc = x - jnp.mean(x, axis=-1, keepdims=True)
    return xc * lax.rsqrt(jnp.mean(xc * xc, axis=-1, keepdims=True) + EPS) * g + b


def _silu(x):
    return x * jax.nn.sigmoid(x)


def _gelu(x):
    return 0.5 * x * (1.0 + lax.erf(x * 0.7071067811865476))


def _lower_bound(blb, layer):
    m = jnp.max(blb, axis=0, keepdims=True)
    e = jnp.exp(blb - m)
    tot = jnp.sum(e, axis=0, keepdims=True)
    acc = jnp.zeros_like(tot)
    for i in range(1, layer + 1):
        acc = acc + e[i:i + 1, :]
    return acc / tot


def _group_rms(o, gavg, g):
    o2 = o * o
    hi = o2.astype(BF16)
    lo = (o2 - hi.astype(F32)).astype(BF16)
    ms = _dot(hi, gavg) + _dot(lo, gavg)
    return o * lax.rsqrt(ms + EPS) * g


def _mod_body(cs_ref, cp_ref, w_ref, b_ref, os_ref, op_ref):
    w = w_ref[...].astype(BF16)
    os_ref[...] = _dot(_silu(cs_ref[...]).astype(BF16), w) + b_ref[...]
    op_ref[...] = _dot(_silu(cp_ref[...]).astype(BF16), w) + b_ref[...]


def _mod_call(c_s, c_p, w_ada, b_ada):
    depth, _, n6 = w_ada.shape
    ns, nb = c_s.shape[0], c_p.shape[0]
    bn = 1536
    return pl.pallas_call(
        _mod_body,
        grid=(depth, n6 // bn),
        in_specs=[
            pl.BlockSpec((ns, D), lambda l, j: (0, 0)),
            pl.BlockSpec((nb, D), lambda l, j: (0, 0)),
            pl.BlockSpec((None, D, bn), lambda l, j: (l, 0, j)),
            pl.BlockSpec((None, 1, bn), lambda l, j: (l, 0, j)),
        ],
        out_specs=[pl.BlockSpec((None, ns, bn), lambda l, j: (l, 0, j)),
                   pl.BlockSpec((None, nb, bn), lambda l, j: (l, 0, j))],
        out_shape=[jax.ShapeDtypeStruct((depth, ns, n6), F32), jax.ShapeDtypeStruct((depth, nb, n6), F32)],
        compiler_params=pltpu.CompilerParams(
            dimension_semantics=("arbitrary", "arbitrary"), vmem_limit_bytes=VMEM_LIMIT),
        name="adaln_mod",
    )(c_s, c_p, w_ada, b_ada.reshape(depth, 1, n6))


def _gmlp_chunk(za, ln_g, ln_b, wcat_b, bias):
    z = _gelu(za)
    u = z[:, :W_A]
    v = _ln(z[:, W_A:], ln_g, ln_b)
    lane_h = lax.broadcasted_iota(jnp.int32, (CHUNK_A, W_A), 1) // DH_A
    vbd = jnp.concatenate(
        [jnp.where(lane_h == h, v, 0.0).astype(BF16) for h in range(HEADS_A)], axis=0)
    mixed = _dot(wcat_b, vbd) + bias
    return u * mixed


def _hgrn_tile(zq, zf, zi, zg, lb, gn_g, st_ref, masks_ref, lmask_ref, bd_ref, gavg, keep):
    nch = TILE_B // SUB
    q = _silu(zq) * (DK ** -0.5)
    f = lb + (1.0 - lb) * jax.nn.sigmoid(zf)
    lf = jnp.log(f)
    k = 1.0 - f

    r15 = lax.broadcasted_iota(jnp.int32, (TILE_B, W_B), 0) & (SUB - 1)
    cs = lf
    for d in (1, 2, 4, 8):
        cs = cs + jnp.where(r15 >= d, pltpu.roll(cs, d, 0), 0.0)
    tot = [cs[SUB * c + SUB - 1:SUB * c + SUB, :] for c in range(nch)]
    tot_b = jnp.concatenate([jnp.broadcast_to(t, (SUB, W_B)) for t in tot], axis=0)
    q_in = q * jnp.exp(cs)
    k_out = k * jnp.exp(tot_b - cs)

    q0, k0 = [], []
    for c in range(nch):
        r = slice(SUB * c, SUB * c + SUB)
        if c % 2 == 0:
            q0.append(q[r] * jnp.exp(cs[r] - tot[c]))
            k0.append(k_out[r])
        else:
            q0.append(q_in[r])
            k0.append(k[r] * jnp.exp(-cs[r]))
    q0 = jnp.concatenate(q0, axis=0)
    k0 = jnp.concatenate(k0, axis=0)

    pre = [jnp.zeros((1, W_B), F32)]
    for c in range(nch):
        pre.append(pre[-1] + tot[c])

    def rows(fn):
        parts = []
        for c in range(nch):
            e = fn(c)
            parts.append(jnp.zeros((SUB, W_B), F32) if e is None
                         else jnp.broadcast_to(jnp.exp(e), (SUB, W_B)))
        return jnp.concatenate(parts, axis=0)

    a_st = rows(lambda c: pre[c])
    b_st = rows(lambda c: pre[nch] - pre[c + 1])
    a_64 = rows(lambda c: pre[c] - pre[4] if c >= 4 else None)
    b_64 = rows(lambda c: pre[4] - pre[c + 1] if c < 4 else None)
    a_32 = rows(lambda c: pre[c] - pre[c & ~3 | 2] if (c & 2) else None)
    b_32 = rows(lambda c: pre[c & ~3 | 2] - pre[c + 1] if not (c & 2) else None)
    decay = jnp.exp(pre[nch])

    m0 = lmask_ref[...] > 0.5
    zb = jnp.zeros((TILE_B, 128), BF16)

    def stack2(x):
        xb = x.astype(BF16)
        return jnp.concatenate([jnp.where(m0, xb, zb), jnp.where(m0, zb, xb)], axis=0)

    levels = ((q0, k0), (q_in * a_32, k_out * b_32), (q_in * a_64, k_out * b_64))
    q_st = q_in * a_st
    k_st = k_out * b_st
    v_t = zi.T
    bd = jnp.logical_and(bd_ref[...] > 0.5, keep)
    decay = jnp.where(keep, decay, 1.0)
    mk = [masks_ref[lv] > 0.5 for lv in range(len(levels))]

    outs = []
    for p in range(HEADS_B // 2):
        sl = slice(128 * p, 128 * p + 128)
        att = jnp.zeros((TILE_B, 2 * TILE_B), BF16)
        for lv, (qa, ka) in reversed(list(enumerate(levels))):
            s = _dot_nt(qa[:, sl].astype(BF16), stack2(ka[:, sl]))
            att = jnp.where(mk[lv], s.astype(BF16), att)
        o = _dot(att, stack2(zi[:, sl]))
        st = st_ref[p]
        o = o + _dot_nt(q_st[:, sl].astype(BF16), st.astype(BF16))
        upd = _dot(v_t[sl, :].astype(BF16), k_st[:, sl].astype(BF16))
        st_ref[p] = st * decay[:, sl] + jnp.where(bd, upd, 0.0)
        outs.append(o)
    o = jnp.concatenate(outs, axis=1)
    return _group_rms(o, gavg, gn_g) * _silu(zg)


def _layerp_body(layer, n_l, n_tiles,
                 x_ref, xn_ref, sh_ref, sc_ref, gt_ref, shn_ref, scn_ref, sh2_ref, sc2_ref, gt2_ref,
                 gpre_ref, gpost_ref, gpre2_ref, gpost2_ref, win_ref, wout_ref, wup_ref, wdn_ref,
                 alg_ref, alb_ref, wcat_ref, abias_ref, blb_ref, gng_ref,
                 cw_ref, cb_ref, clg_ref, clb_ref, masks_ref, lmask_ref, bd_ref, gavg_ref,
                 o_ref, hs_ref, cv_ref,
                 z_ref, xx_ref, xs_ref, y_ref, x1_ref, hm_ref, h_ref, st_ref, tail_ref):
    i = pl.program_id(0)
    keep = i < n_tiles

    @pl.when(jnp.logical_and(i % n_l == 0, keep))
    def _():
        st_ref[...] = jnp.zeros_like(st_ref)
        tail_ref[...] = jnp.zeros_like(tail_ref)

    @pl.when(i == 0)
    def _():
        x1_ref[1] = jnp.zeros((TL, D), F32)
        hm_ref[...] = jnp.zeros((TL, D), BF16)
        h_ref[...] = (_rms(x_ref[...], gpre_ref[...] * (1.0 + sc_ref[...])) + sh_ref[...]).astype(BF16)

    cwid = D_FF // 4

    def mlp_chunk(c, acc):
        u = jnp.maximum(_dot(hm_ref[...], wup_ref[:, c * cwid:(c + 1) * cwid]), 0.0)
        part = _dot((u * u).astype(BF16), wdn_ref[c * cwid:(c + 1) * cwid, :])
        return part if acc is None else acc + part

    z_ref[:, OFF_C:N_IN] = _dot(h_ref[...], win_ref[:, OFF_C:N_IN])
    z_ref[:, 0:OFF_C] = _dot(h_ref[...], win_ref[:, 0:OFF_C])

    ym = mlp_chunk(0, None)

    xg = z_ref[:, OFF_C:OFF_C + W_C] * jax.nn.sigmoid(z_ref[:, OFF_C + W_C:N_IN])
    old_tail = tail_ref[...]
    xx_ref[0:TAIL, :] = old_tail
    xx_ref[TAIL:TAIL + TL, :] = xg
    tail_ref[...] = jnp.where(keep, xg[TL - TAIL:TL, :], old_tail)
    span = TL + TAIL - 8
    for r in range(1, 8):
        xs_ref[r - 1] = xx_ref[pl.ds(r, span), :]
    rb = 64
    for c in range(TL // rb):
        acc = jnp.zeros((rb, W_C), F32)
        for t in range(TAPS):
            a, r = divmod(TAIL - (TAPS - 1) + t, 8)
            src = xx_ref[pl.ds(c * rb + 8 * a, rb), :] if r == 0 else xs_ref[r - 1, pl.ds(c * rb + 8 * a, rb), :]
            acc = acc + src * cw_ref[pl.ds(t, 1), :]
        yc = _silu(_ln(acc + cb_ref[...], clg_ref[...], clb_ref[...]))
        y_ref[c * rb:(c + 1) * rb, W_A + W_B:D] = yc.astype(BF16)

    lb = _lower_bound(blb_ref[...], layer)
    gavg = gavg_ref[...]
    for t in range(TL // TILE_B):
        ym = mlp_chunk(1 + t, ym)
        r = slice(t * TILE_B, (t + 1) * TILE_B)
        yb = _hgrn_tile(z_ref[r, OFF_Q:OFF_F], z_ref[r, OFF_F:OFF_I], z_ref[r, OFF_I:OFF_G],
                        z_ref[r, OFF_G:OFF_C], lb, gng_ref[...], st_ref, masks_ref, lmask_ref, bd_ref, gavg,
                        keep)
        y_ref[r, W_A:W_A + W_B] = yb.astype(BF16)

    tri = (lax.broadcasted_iota(jnp.int32, (CHUNK_A, HEADS_A * CHUNK_A), 1) & (CHUNK_A - 1)
           ) <= lax.broadcasted_iota(jnp.int32, (CHUNK_A, HEADS_A * CHUNK_A), 0)
    wcat_b = jnp.where(tri, wcat_ref[...], 0.0).astype(BF16)
    for c in range(TL // CHUNK_A):
        r = slice(c * CHUNK_A, (c + 1) * CHUNK_A)
        ya = _gmlp_chunk(z_ref[r, OFF_A:OFF_Q], alg_ref[...], alb_ref[...], wcat_b, abias_ref[...])
        y_ref[r, 0:W_A] = ya.astype(BF16)

    ym = mlp_chunk(3, ym)

    o_ref[...] = x1_ref[(i + 1) % 2] + _rms(ym, gt2_ref[...] * gpost2_ref[...])
    y = _dot(y_ref[...], wout_ref[...])
    x1 = x_ref[...] + _rms(y, gt_ref[...] * gpost_ref[...])
    x1_ref[i % 2] = x1
    hm_ref[...] = (_rms(x1, gpre2_ref[...] * (1.0 + sc2_ref[...])) + sh2_ref[...]).astype(BF16)
    h_ref[...] = (_rms(xn_ref[...], gpre_ref[...] * (1.0 + scn_ref[...])) + shn_ref[...]).astype(BF16)

    @pl.when(jnp.logical_and(i % n_l == n_l - 1, keep))
    def _():
        for p in range(HEADS_B // 2):
            t = st_ref[p].T
            hs_ref[2 * p] = t[0:DK, 0:DV]
            hs_ref[2 * p + 1] = t[DK:2 * DK, DV:2 * DV]
        cv_ref[...] = tail_ref[TAIL - (TAPS - 1):TAIL, :]


def _level_masks():
    t = np.arange(TILE_B)[:, None]
    s = np.arange(2 * TILE_B)[None, :] % TILE_B
    ct, cs = t // SUB, s // SUB
    m = [
        (ct // 2 == cs // 2) & (s <= t),
        (ct // 2 == cs // 2 + 1) & ((ct // 2) % 2 == 1),
        (ct // 4 == cs // 4 + 1) & ((ct // 4) % 2 == 1),
    ]
    return np.stack(m).astype(np.float32)


def _first_head_lanes():
    return np.broadcast_to(np.arange(128)[None, :] < DK, (TILE_B, 128)).astype(np.float32)


def _block_diag_mask():
    i = np.arange(128)
    return ((i[:, None] // DK) == (i[None, :] // DK)).astype(np.float32)


def _group_avg():
    i = np.arange(W_B)
    return ((i[:, None] // DV) == (i[None, :] // DV)).astype(np.float32) / DV


def _full(shape):
    nd = len(shape)
    return pl.BlockSpec(shape, lambda *_: (0,) * nd)


def _layerp_call(layer, x, modp, gpre, gpost, gpre2, gpost2, win_b, wout_b, wup_b, wdn_b,
                 alg, alb, wcat, abias, blb, gng, cw, cb, clg, clb, masks, lmask, bdm, gavg):
    nb, L, _ = x.shape
    n_l = L // TL
    n_tiles = nb * n_l
    row = lambda a: a.reshape(1, -1)
    cur = lambda i: jnp.minimum(i, n_tiles - 1)
    prv = lambda i: jnp.maximum(i - 1, 0)
    nxt = lambda i: jnp.minimum(i + 1, n_tiles - 1)
    mcur = lambda kk: pl.BlockSpec((None, None, 1, D), lambda i: (layer, cur(i) // n_l, 0, kk))
    mprv = lambda kk: pl.BlockSpec((None, None, 1, D), lambda i: (layer, prv(i) // n_l, 0, kk))
    mnxt = lambda kk: pl.BlockSpec((None, None, 1, D), lambda i: (layer, nxt(i) // n_l, 0, kk))
    once = lambda shape: pl.BlockSpec((None,) + shape, lambda i: (layer,) + (0,) * len(shape),
                                      pipeline_mode=pl.Buffered(1))
    return pl.pallas_call(
        functools.partial(_layerp_body, layer, n_l, n_tiles),
        grid=(n_tiles + 1,),
        in_specs=[
            pl.BlockSpec((None, TL, D), lambda i: (cur(i) // n_l, cur(i) % n_l, 0)),
            pl.BlockSpec((None, TL, D), lambda i: (nxt(i) // n_l, nxt(i) % n_l, 0)),
            mcur(0), mcur(1), mcur(2), mnxt(0), mnxt(1), mcur(3), mcur(4), mprv(5),
            _full((1, D)), _full((1, D)), _full((1, D)), _full((1, D)),
            once((D, N_IN)), once((D, D)), once((D, D_FF)), once((D_FF, D)),
            _full((1, W_A)), _full((1, W_A)), _full((CHUNK_A, HEADS_A * CHUNK_A)), _full((CHUNK_A, W_A)),
            _full(blb.shape), _full((1, W_B)),
            _full((TAPS, W_C)), _full((1, W_C)), _full((1, W_C)), _full((1, W_C)),
            _full(masks.shape), _full(lmask.shape), _full(bdm.shape), _full(gavg.shape),
        ],
        out_specs=[
            pl.BlockSpec((None, TL, D), lambda i: (prv(i) // n_l, prv(i) % n_l, 0)),
            pl.BlockSpec((None, HEADS_B, DK, DV), lambda i: (cur(i) // n_l, 0, 0, 0)),
            pl.BlockSpec((None, TAPS - 1, W_C), lambda i: (cur(i) // n_l, 0, 0)),
        ],
        out_shape=[
            jax.ShapeDtypeStruct((nb, L, D), F32),
            jax.ShapeDtypeStruct((nb, HEADS_B, DK, DV), F32),
            jax.ShapeDtypeStruct((nb, TAPS - 1, W_C), F32),
        ],
        scratch_shapes=[
            pltpu.VMEM((TL, N_IN), F32),
            pltpu.VMEM((TAIL + TL, W_C), F32),
            pltpu.VMEM((7, TAIL + TL - 8, W_C), F32),
            pltpu.VMEM((TL, D), BF16),
            pltpu.VMEM((2, TL, D), F32),
            pltpu.VMEM((TL, D), BF16),
            pltpu.VMEM((TL, D), BF16),
            pltpu.VMEM((HEADS_B // 2, 128, 128), F32),
            pltpu.VMEM((TAIL, W_C), F32),
        ],
        compiler_params=pltpu.CompilerParams(
            dimension_semantics=("arbitrary",), vmem_limit_bytes=VMEM_LIMIT),
        name=f"layer_prompt_l{layer}",
    )(x, x, modp, modp, modp, modp, modp, modp, modp, modp, row(gpre), row(gpost), row(gpre2), row(gpost2),
      win_b, wout_b, wup_b, wdn_b, row(alg), row(alb), wcat, abias,
      blb, row(gng), cw, row(cb), row(clg), row(clb), masks, lmask, bdm, gavg)


def _mixs_body(layer, nt,
               x_ref, sh_ref, sc_ref, gt_ref, gpre_ref, gpost_ref, win_ref, wout_ref,
               alg_ref, alb_ref, acoef_ref, abias_ref, blb_ref, gng_ref,
               cw_ref, cb_ref, clg_ref, clb_ref, gavg_ref, cst_ref, s_ref,
               o_ref, so_ref, xg_ref, v_ref,
               z_ref, y_ref, ft_ref, kt_ref, qt_ref, vt_ref, ot_ref):
    hd = pl.program_id(0)
    ns = x_ref.shape[1]

    @pl.when(hd == 0)
    def _():
        x = x_ref[...]
        h = _rms(x, gpre_ref[...]) * (1.0 + sc_ref[...]) + sh_ref[...]
        z_ref[...] = _dot(h.reshape(nt * ns, D).astype(BF16), win_ref[...])

        za = _gelu(z_ref[:, OFF_A:OFF_Q])
        v = _ln(za[:, W_A:], alg_ref[...], alb_ref[...])
        for t in range(nt):
            mixed = abias_ref[pl.ds(t, 1), :]
            for s in range(t + 1):
                mixed = mixed + acoef_ref[pl.ds(t * nt + s, 1), :] * v[s * ns:(s + 1) * ns, :]
            y_ref[t * ns:(t + 1) * ns, 0:W_A] = (za[t * ns:(t + 1) * ns, :W_A] * mixed).astype(BF16)
            v_ref[t] = v[t * ns:(t + 1) * ns, :]

        xg = z_ref[:, OFF_C:OFF_C + W_C] * jax.nn.sigmoid(z_ref[:, OFF_C + W_C:N_IN])
        hist = TAPS - 1
        for t in range(nt):
            xg_ref[t] = xg[t * ns:(t + 1) * ns, :]
        for t in range(nt):
            acc = jnp.zeros((ns, W_C), F32)
            for tap in range(TAPS):
                pos = t + tap
                src = cst_ref[pos] if pos < hist else xg[(pos - hist) * ns:(pos - hist + 1) * ns, :]
                acc = acc + src * cw_ref[pl.ds(tap, 1), :]
            yc = _silu(_ln(acc + cb_ref[...], clg_ref[...], clb_ref[...]))
            y_ref[t * ns:(t + 1) * ns, W_A + W_B:D] = yc.astype(BF16)

        lb = _lower_bound(blb_ref[...], layer)
        for t in range(nt):
            r = slice(t * ns, (t + 1) * ns)
            f = lb + (1.0 - lb) * jax.nn.sigmoid(z_ref[r, OFF_F:OFF_I])
            ft_ref[t] = f.T
            kt_ref[t] = (1.0 - f).T
            qt_ref[t] = (_silu(z_ref[r, OFF_Q:OFF_F]) * (DK ** -0.5)).T
            vt_ref[t] = z_ref[r, OFF_I:OFF_G].T

    base = pl.multiple_of(hd * DK, DK)
    vts = [vt_ref[t, pl.ds(base, DV), :] for t in range(nt)]

    def krow(i, accs):
        srow = s_ref[i]
        new = []
        for t in range(nt):
            srow = ft_ref[t, pl.ds(base + i, 1), :] * srow + kt_ref[t, pl.ds(base + i, 1), :] * vts[t]
            new.append(accs[t] + qt_ref[t, pl.ds(base + i, 1), :] * srow)
        so_ref[i] = srow
        return tuple(new)

    accs = lax.fori_loop(0, DK, krow, tuple(jnp.zeros((DV, ns), F32) for _ in range(nt)))
    for t in range(nt):
        ot_ref[t, pl.ds(base, DV), :] = accs[t]

    @pl.when(hd == HEADS_B - 1)
    def _():
        gavg = gavg_ref[...]
        for t in range(nt):
            r = slice(t * ns, (t + 1) * ns)
            o = ot_ref[t].T
            yb = _group_rms(o, gavg, gng_ref[...]) * _silu(z_ref[r, OFF_G:OFF_C])
            y_ref[r, W_A:W_A + W_B] = yb.astype(BF16)
        y = _dot(y_ref[...], wout_ref[...]).reshape(nt, ns, D)
        o_ref[...] = x_ref[...] + gt_ref[...] * _rms(y, gpost_ref[...])


def _mixs_call(layer, xs, mods, gpre, gpost, win_b, wout_b, alg, alb, acoef, abias, blb, gng,
               cw, cb, clg, clb, gavg, cst, s_in):
    nt, ns, _ = xs.shape
    row = lambda a: a.reshape(1, -1)
    mspec = lambda kk: pl.BlockSpec((None, ns, D), lambda h: (layer, 0, kk))
    hist = TAPS - 1
    return pl.pallas_call(
        functools.partial(_mixs_body, layer, nt),
        grid=(HEADS_B,),
        in_specs=[
            _full((nt, ns, D)),
            mspec(0), mspec(1), mspec(2),
            _full((1, D)), _full((1, D)),
            pl.BlockSpec((None, D, N_IN), lambda h: (layer, 0, 0)), pl.BlockSpec((None, D, D), lambda h: (layer, 0, 0)),
            _full((1, W_A)), _full((1, W_A)), _full(acoef.shape), _full(abias.shape),
            _full(blb.shape), _full((1, W_B)),
            _full((TAPS, W_C)), _full((1, W_C)), _full((1, W_C)), _full((1, W_C)),
            _full(gavg.shape), _full((hist, ns, W_C)),
            pl.BlockSpec((None, DK, DV, ns), lambda h: (h, 0, 0, 0)),
        ],
        out_specs=[
            _full((nt, ns, D)),
            pl.BlockSpec((None, DK, DV, ns), lambda h: (h, 0, 0, 0)),
            _full((nt, ns, W_C)),
            _full((nt, ns, W_A)),
        ],
        out_shape=[
            jax.ShapeDtypeStruct((nt, ns, D), F32),
            jax.ShapeDtypeStruct((HEADS_B, DK, DV, ns), F32),
            jax.ShapeDtypeStruct((nt, ns, W_C), F32),
            jax.ShapeDtypeStruct((nt, ns, W_A), F32),
        ],
        scratch_shapes=[
            pltpu.VMEM((nt * ns, N_IN), F32),
            pltpu.VMEM((nt * ns, D), BF16),
            pltpu.VMEM((nt, W_B, ns), F32),
            pltpu.VMEM((nt, W_B, ns), F32),
            pltpu.VMEM((nt, W_B, ns), F32),
            pltpu.VMEM((nt, W_B, ns), F32),
            pltpu.VMEM((nt, W_B, ns), F32),
        ],
        compiler_params=pltpu.CompilerParams(
            dimension_semantics=("arbitrary",), vmem_limit_bytes=VMEM_LIMIT),
        name=f"mix_sample_l{layer}",
    )(xs, mods, mods, mods, row(gpre), row(gpost), win_b, wout_b, row(alg), row(alb), acoef, abias,
      blb, row(gng), cw, row(cb), row(clg), row(clb), gavg, cst, s_in)


def _mlps_body(x_ref, sh_ref, sc_ref, gt_ref, gpre_ref, gpost_ref, wup_ref, wdn_ref, o_ref, h_ref, acc_ref):
    c = pl.program_id(0)
    nt, ns, _ = x_ref.shape

    @pl.when(c == 0)
    def _():
        h = _rms(x_ref[...], gpre_ref[...] * (1.0 + sc_ref[...])) + sh_ref[...]
        h_ref[...] = h.reshape(nt * ns, D).astype(BF16)
        acc_ref[...] = jnp.zeros_like(acc_ref)

    u = jnp.maximum(_dot(h_ref[...], wup_ref[...]), 0.0)
    acc_ref[...] += _dot((u * u).astype(BF16), wdn_ref[...])

    @pl.when(c == pl.num_programs(0) - 1)
    def _():
        o_ref[...] = x_ref[...] + _rms(acc_ref[...].reshape(nt, ns, D), gt_ref[...] * gpost_ref[...])


def _mlps_call(layer, x, mods, gpre, gpost, wup_b, wdn_b):
    nt, ns, _ = x.shape
    row = lambda v: v.reshape(1, -1)
    cw = D_FF // 4
    mspec = lambda kk: pl.BlockSpec((None, ns, D), lambda c: (layer, 0, kk))
    return pl.pallas_call(
        _mlps_body,
        grid=(D_FF // cw,),
        in_specs=[_full((nt, ns, D)), mspec(3), mspec(4), mspec(5), _full((1, D)), _full((1, D)),
                  pl.BlockSpec((None, D, cw), lambda c: (layer, 0, c)),
                  pl.BlockSpec((None, cw, D), lambda c: (layer, c, 0))],
        out_specs=_full((nt, ns, D)),
        out_shape=jax.ShapeDtypeStruct(x.shape, F32),
        scratch_shapes=[pltpu.VMEM((nt * ns, D), BF16), pltpu.VMEM((nt * ns, D), F32)],
        compiler_params=pltpu.CompilerParams(
            dimension_semantics=("arbitrary",), vmem_limit_bytes=VMEM_LIMIT),
        name=f"mlp_sample_l{layer}",
    )(x, mods, mods, mods, row(gpre), row(gpost), wup_b, wdn_b)


def kernel(x_prompt, x_sample, state_hgrn, state_conv, c_prompt, c_sample, w_ada, b_ada, g_pre_mix, g_post_mix, g_pre_mlp, g_post_mlp, w_in, a_ln_g, a_ln_b, a_w_s, a_b_s, b_lb, b_gn_g, c_w_dw, c_b_dw, c_ln_g, c_ln_b, w_out, w_up, w_down):
    depth = w_in.shape[0]
    nb = x_prompt.shape[0]
    ns, nt, _ = x_sample.shape
    mod_s, mod_p = _mod_call(c_sample, c_prompt, w_ada, b_ada)
    modp = mod_p.reshape(depth, nb, 1, 6 * D)
    masks = jnp.asarray(_level_masks(), dtype=BF16)
    lmask = jnp.asarray(_first_head_lanes(), dtype=BF16)
    bdm = jnp.asarray(_block_diag_mask())
    gavg = jnp.asarray(_group_avg(), dtype=BF16)
    win_b, wout_b, wup_b, wdn_b = (w.astype(BF16) for w in (w_in, w_out, w_up, w_down))

    xp = x_prompt
    xs = jnp.transpose(x_sample, (1, 0, 2))
    hg_p, hg_s, cv_p, cv_s, v_s = [], [], [], [], []
    for l in range(depth):
        wcat = jnp.transpose(a_w_s[l], (1, 0, 2)).reshape(CHUNK_A, HEADS_A * CHUNK_A)
        abias = jnp.repeat(a_b_s[l].T, DH_A, axis=1)
        acoef = jnp.repeat(jnp.transpose(a_w_s[l][:, :nt, :nt], (1, 2, 0)).reshape(nt * nt, HEADS_A),
                           DH_A, axis=1)

        xp, hs_l, cv_l = _layerp_call(
            l, xp, modp, g_pre_mix[l], g_post_mix[l], g_pre_mlp[l], g_post_mlp[l], win_b, wout_b, wup_b, wdn_b,
            a_ln_g[l], a_ln_b[l], wcat, abias, b_lb, b_gn_g[l], c_w_dw[l], c_b_dw[l], c_ln_g[l], c_ln_b[l],
            masks, lmask, bdm, gavg)

        cst = jnp.transpose(state_conv[l], (1, 0, 2))
        s_in = jnp.transpose(state_hgrn[l], (1, 2, 3, 0))
        xs, s_out, xg_s, vrow = _mixs_call(
            l, xs, mod_s, g_pre_mix[l], g_post_mix[l], win_b, wout_b, a_ln_g[l], a_ln_b[l], acoef,
            abias[:nt], b_lb, b_gn_g[l], c_w_dw[l], c_b_dw[l], c_ln_g[l], c_ln_b[l], gavg, cst, s_in)
        xs = _mlps_call(l, xs, mod_s, g_pre_mlp[l], g_post_mlp[l], wup_b, wdn_b)

        hg_p.append(hs_l)
        hg_s.append(jnp.transpose(s_out, (3, 0, 1, 2)))
        cv_p.append(cv_l)
        cv_s.append(jnp.concatenate([state_conv[l][:, nt:, :], jnp.transpose(xg_s, (1, 0, 2))], axis=1))
        v_s.append(jnp.transpose(vrow, (1, 0, 2)))

    return (xp, jnp.transpose(xs, (1, 0, 2)), jnp.stack(hg_p), jnp.stack(hg_s),
            jnp.stack(cv_p), jnp.stack(cv_s), jnp.stack(v_s))
```

```python
import functools

import numpy as np
import jax
import jax.numpy as jnp
from jax import lax
from jax.experimental import pallas as pl
from jax.experimental.pallas import tpu as pltpu

F32 = jnp.float32
BF16 = jnp.bfloat16

D = 1024
HEADS_A, DH_A, W_A, CHUNK_A = 4, 64, 256, 128
HEADS_B, DK, DV, W_B = 8, 64, 64, 512
SUB = 16
TILE_B = 128
W_C, TAPS = 256, 31
TAIL = 32
D_FF = 4096
N_IN = 3072
EPS = 1e-6
OFF_A, OFF_Q, OFF_F, OFF_I, OFF_G, OFF_C = 0, 512, 1024, 1536, 2048, 2560

TL = 512
MLP_CHUNKS = 8
VMEM_LIMIT = 60 * 1024 * 1024


def _dot(a, b):
    return jnp.dot(a, b, preferred_element_type=F32)


def _dot_nt(a, b):
    return lax.dot_general(a, b, (((1,), (1,)), ((), ())), preferred_element_type=F32)


def _rms(x, g):
    return x * lax.rsqrt(jnp.mean(x * x, axis=-1, keepdims=True) + EPS) * g


def _ln(x, g, b):
    xc = x - jnp.mean(x, axis=-1, keepdims=True)
    return xc * lax.rsqrt(jnp.mean(xc * xc, axis=-1, keepdims=True) + EPS) * g + b


def _silu(x):
    return x * jax.nn.sigmoid(x)


def _gelu(x):
    return 0.5 * x * (1.0 + lax.erf(x * 0.7071067811865476))


def _lower_bound(blb, layer):
    m = jnp.max(blb, axis=0, keepdims=True)
    e = jnp.exp(blb - m)
    tot = jnp.sum(e, axis=0, keepdims=True)
    acc = jnp.zeros_like(tot)
    for i in range(1, layer + 1):
        acc = acc + e[i:i + 1, :]
    return acc / tot


def _group_rms(o, gavg, g):
    o2 = o * o
    hi = o2.astype(BF16)
    lo = (o2 - hi.astype(F32)).astype(BF16)
    ms = _dot(hi, gavg) + _dot(lo, gavg)
    return o * lax.rsqrt(ms + EPS) * g


def _mod_body(cs_ref, cp_ref, w_ref, b_ref, os_ref, op_ref):
    w = w_ref[...].astype(BF16)
    os_ref[...] = _dot(_silu(cs_ref[...]).astype(BF16), w) + b_ref[...]
    op_ref[...] = _dot(_silu(cp_ref[...]).astype(BF16), w) + b_ref[...]


def _mod_call(c_s, c_p, w_ada, b_ada):
    depth, _, n6 = w_ada.shape
    ns, nb = c_s.shape[0], c_p.shape[0]
    bn = 1536
    return pl.pallas_call(
        _mod_body,
        grid=(depth, n6 // bn),
        in_specs=[
            pl.BlockSpec((ns, D), lambda l, j: (0, 0)),
            pl.BlockSpec((nb, D), lambda l, j: (0, 0)),
            pl.BlockSpec((None, D, bn), lambda l, j: (l, 0, j)),
            pl.BlockSpec((None, 1, bn), lambda l, j: (l, 0, j)),
        ],
        out_specs=[pl.BlockSpec((None, ns, bn), lambda l, j: (l, 0, j)),
                   pl.BlockSpec((None, nb, bn), lambda l, j: (l, 0, j))],
        out_shape=[jax.ShapeDtypeStruct((depth, ns, n6), F32), jax.ShapeDtypeStruct((depth, nb, n6), F32)],
        compiler_params=pltpu.CompilerParams(
            dimension_semantics=("arbitrary", "arbitrary"), vmem_limit_bytes=VMEM_LIMIT),
        name="adaln_mod",
    )(c_s, c_p, w_ada, b_ada.reshape(depth, 1, n6))


def _gmlp_chunk(za, ln_g, ln_b, wcat_b, bias):
    z = _gelu(za)
    u = z[:, :W_A]
    v = _ln(z[:, W_A:], ln_g, ln_b)
    lane_h = lax.broadcasted_iota(jnp.int32, (CHUNK_A, W_A), 1) // DH_A
    vbd = jnp.concatenate(
        [jnp.where(lane_h == h, v, 0.0).astype(BF16) for h in range(HEADS_A)], axis=0)
    mixed = _dot(wcat_b, vbd) + bias
    return u * mixed


def _hgrn_tile(zq, zf, zi, zg, lb, gn_g, st_ref, masks_ref, lmask_ref, bd_ref, gavg, keep):
    nch = TILE_B // SUB
    q = _silu(zq) * (DK ** -0.5)
    f = lb + (1.0 - lb) * jax.nn.sigmoid(zf)
    lf = jnp.log(f)
    k = 1.0 - f

    r15 = lax.broadcasted_iota(jnp.int32, (TILE_B, W_B), 0) & (SUB - 1)
    cs = lf
    for d in (1, 2, 4, 8):
        cs = cs + jnp.where(r15 >= d, pltpu.roll(cs, d, 0), 0.0)
    tot = [cs[SUB * c + SUB - 1:SUB * c + SUB, :] for c in range(nch)]
    tot_b = jnp.concatenate([jnp.broadcast_to(t, (SUB, W_B)) for t in tot], axis=0)
    q_in = q * jnp.exp(cs)
    k_out = k * jnp.exp(tot_b - cs)

    q0, k0 = [], []
    for c in range(nch):
        r = slice(SUB * c, SUB * c + SUB)
        if c % 2 == 0:
            q0.append(q[r] * jnp.exp(cs[r] - tot[c]))
            k0.append(k_out[r])
        else:
            q0.append(q_in[r])
            k0.append(k[r] * jnp.exp(-cs[r]))
    q0 = jnp.concatenate(q0, axis=0)
    k0 = jnp.concatenate(k0, axis=0)

    pre = [jnp.zeros((1, W_B), F32)]
    for c in range(nch):
        pre.append(pre[-1] + tot[c])

    def rows(fn):
        parts = []
        for c in range(nch):
            e = fn(c)
            parts.append(jnp.zeros((SUB, W_B), F32) if e is None
                         else jnp.broadcast_to(jnp.exp(e), (SUB, W_B)))
        return jnp.concatenate(parts, axis=0)

    a_st = rows(lambda c: pre[c])
    b_st = rows(lambda c: pre[nch] - pre[c + 1])
    a_64 = rows(lambda c: pre[c] - pre[4] if c >= 4 else None)
    b_64 = rows(lambda c: pre[4] - pre[c + 1] if c < 4 else None)
    a_32 = rows(lambda c: pre[c] - pre[c & ~3 | 2] if (c & 2) else None)
    b_32 = rows(lambda c: pre[c & ~3 | 2] - pre[c + 1] if not (c & 2) else None)
    decay = jnp.exp(pre[nch])

    m0 = lmask_ref[...] > 0.5
    zb = jnp.zeros((TILE_B, 128), BF16)

    def stack2(x):
        xb = x.astype(BF16)
        return jnp.concatenate([jnp.where(m0, xb, zb), jnp.where(m0, zb, xb)], axis=0)

    levels = ((q0, k0), (q_in * a_32, k_out * b_32), (q_in * a_64, k_out * b_64))
    q_st = q_in * a_st
    k_st = k_out * b_st
    v_t = zi.T
    bd = jnp.logical_and(bd_ref[...] > 0.5, keep)
    decay = jnp.where(keep, decay, 1.0)
    mk = [masks_ref[lv] > 0.5 for lv in range(len(levels))]

    outs = []
    for p in range(HEADS_B // 2):
        sl = slice(128 * p, 128 * p + 128)
        att = jnp.zeros((TILE_B, 2 * TILE_B), BF16)
        for lv, (qa, ka) in reversed(list(enumerate(levels))):
            s = _dot_nt(qa[:, sl].astype(BF16), stack2(ka[:, sl]))
            att = jnp.where(mk[lv], s.astype(BF16), att)
        o = _dot(att, stack2(zi[:, sl]))
        st = st_ref[p]
        o = o + _dot_nt(q_st[:, sl].astype(BF16), st.astype(BF16))
        upd = _dot(v_t[sl, :].astype(BF16), k_st[:, sl].astype(BF16))
        st_ref[p] = st * decay[:, sl] + jnp.where(bd, upd, 0.0)
        outs.append(o)
    o = jnp.concatenate(outs, axis=1)
    return _group_rms(o, gavg, gn_g) * _silu(zg)


def _layerp_body(layer, n_l, n_tiles,
                 x_ref, xn_ref, sh_ref, sc_ref, gt_ref, shn_ref, scn_ref, sh2_ref, sc2_ref, gt2_ref,
                 gpre_ref, gpost_ref, gpre2_ref, gpost2_ref, win_ref, wout_ref, wup_ref, wdn_ref,
                 alg_ref, alb_ref, wcat_ref, abias_ref, blb_ref, gng_ref,
                 cw_ref, cb_ref, clg_ref, clb_ref, masks_ref, lmask_ref, bd_ref, gavg_ref,
                 o_ref, hs_ref, cv_ref,
                 z_ref, xx_ref, xs_ref, y_ref, x1_ref, hm_ref, h_ref, st_ref, tail_ref):
    i = pl.program_id(0)
    keep = i < n_tiles

    @pl.when(jnp.logical_and(i % n_l == 0, keep))
    def _():
        st_ref[...] = jnp.zeros_like(st_ref)
        tail_ref[...] = jnp.zeros_like(tail_ref)

    @pl.when(i == 0)
    def _():
        x1_ref[1] = jnp.zeros((TL, D), F32)
        hm_ref[...] = jnp.zeros((TL, D), BF16)
        h_ref[...] = (_rms(x_ref[...], gpre_ref[...] * (1.0 + sc_ref[...])) + sh_ref[...]).astype(BF16)

    cwid = D_FF // MLP_CHUNKS

    def mlp_chunk(c, acc):
        u = jnp.maximum(_dot(hm_ref[...], wup_ref[:, c * cwid:(c + 1) * cwid]), 0.0)
        part = _dot((u * u).astype(BF16), wdn_ref[c * cwid:(c + 1) * cwid, :])
        return part if acc is None else acc + part

    z_ref[:, OFF_C:N_IN] = _dot(h_ref[...], win_ref[:, OFF_C:N_IN])
    z_ref[:, 0:OFF_C] = _dot(h_ref[...], win_ref[:, 0:OFF_C])

    ym = mlp_chunk(0, None)

    xg = z_ref[:, OFF_C:OFF_C + W_C] * jax.nn.sigmoid(z_ref[:, OFF_C + W_C:N_IN])
    old_tail = tail_ref[...]
    xx_ref[0:TAIL, :] = old_tail
    xx_ref[TAIL:TAIL + TL, :] = xg
    tail_ref[...] = jnp.where(keep, xg[TL - TAIL:TL, :], old_tail)
    span = TL + TAIL - 8
    for r in range(1, 8):
        xs_ref[r - 1] = xx_ref[pl.ds(r, span), :]
    rb = 64
    for c in range(TL // rb):
        acc = jnp.zeros((rb, W_C), F32)
        for t in range(TAPS):
            a, r = divmod(TAIL - (TAPS - 1) + t, 8)
            src = xx_ref[pl.ds(c * rb + 8 * a, rb), :] if r == 0 else xs_ref[r - 1, pl.ds(c * rb + 8 * a, rb), :]
            acc = acc + src * cw_ref[pl.ds(t, 1), :]
        yc = _silu(_ln(acc + cb_ref[...], clg_ref[...], clb_ref[...]))
        y_ref[c * rb:(c + 1) * rb, W_A + W_B:D] = yc.astype(BF16)

    lb = _lower_bound(blb_ref[...], layer)
    gavg = gavg_ref[...]
    for t in range(TL // TILE_B):
        ym = mlp_chunk(1 + t, ym)
        r = slice(t * TILE_B, (t + 1) * TILE_B)
        yb = _hgrn_tile(z_ref[r, OFF_Q:OFF_F], z_ref[r, OFF_F:OFF_I], z_ref[r, OFF_I:OFF_G],
                        z_ref[r, OFF_G:OFF_C], lb, gng_ref[...], st_ref, masks_ref, lmask_ref, bd_ref, gavg,
                        keep)
        y_ref[r, W_A:W_A + W_B] = yb.astype(BF16)

    tri = (lax.broadcasted_iota(jnp.int32, (CHUNK_A, HEADS_A * CHUNK_A), 1) & (CHUNK_A - 1)
           ) <= lax.broadcasted_iota(jnp.int32, (CHUNK_A, HEADS_A * CHUNK_A), 0)
    wcat_b = jnp.where(tri, wcat_ref[...], 0.0).astype(BF16)
    for c in range(TL // CHUNK_A):
        r = slice(c * CHUNK_A, (c + 1) * CHUNK_A)
        ya = _gmlp_chunk(z_ref[r, OFF_A:OFF_Q], alg_ref[...], alb_ref[...], wcat_b, abias_ref[...])
        y_ref[r, 0:W_A] = ya.astype(BF16)

    for c in range(1 + TL // TILE_B, MLP_CHUNKS):
        ym = mlp_chunk(c, ym)

    o_ref[...] = x1_ref[(i + 1) % 2] + _rms(ym, gt2_ref[...] * gpost2_ref[...])
    y = _dot(y_ref[...], wout_ref[...])
    x1 = x_ref[...] + _rms(y, gt_ref[...] * gpost_ref[...])
    x1_ref[i % 2] = x1
    hm_ref[...] = (_rms(x1, gpre2_ref[...] * (1.0 + sc2_ref[...])) + sh2_ref[...]).astype(BF16)
    h_ref[...] = (_rms(xn_ref[...], gpre_ref[...] * (1.0 + scn_ref[...])) + shn_ref[...]).astype(BF16)

    @pl.when(jnp.logical_and(i % n_l == n_l - 1, keep))
    def _():
        for p in range(HEADS_B // 2):
            t = st_ref[p].T
            hs_ref[2 * p] = t[0:DK, 0:DV]
            hs_ref[2 * p + 1] = t[DK:2 * DK, DV:2 * DV]
        cv_ref[...] = tail_ref[TAIL - (TAPS - 1):TAIL, :]


def _level_masks():
    t = np.arange(TILE_B)[:, None]
    s = np.arange(2 * TILE_B)[None, :] % TILE_B
    ct, cs = t // SUB, s // SUB
    m = [
        (ct // 2 == cs // 2) & (s <= t),
        (ct // 2 == cs // 2 + 1) & ((ct // 2) % 2 == 1),
        (ct // 4 == cs // 4 + 1) & ((ct // 4) % 2 == 1),
    ]
    return np.stack(m).astype(np.float32)


def _first_head_lanes():
    return np.broadcast_to(np.arange(128)[None, :] < DK, (TILE_B, 128)).astype(np.float32)


def _block_diag_mask():
    i = np.arange(128)
    return ((i[:, None] // DK) == (i[None, :] // DK)).astype(np.float32)


def _group_avg():
    i = np.arange(W_B)
    return ((i[:, None] // DV) == (i[None, :] // DV)).astype(np.float32) / DV


def _full(shape):
    nd = len(shape)
    return pl.BlockSpec(shape, lambda *_: (0,) * nd)


def _layerp_call(layer, x, modp, gpre, gpost, gpre2, gpost2, win_b, wout_b, wup_b, wdn_b,
                 alg, alb, wcat, abias, blb, gng, cw, cb, clg, clb, masks, lmask, bdm, gavg):
    nb, L, _ = x.shape
    n_l = L // TL
    n_tiles = nb * n_l
    row = lambda a: a.reshape(1, -1)
    cur = lambda i: jnp.minimum(i, n_tiles - 1)
    prv = lambda i: jnp.maximum(i - 1, 0)
    nxt = lambda i: jnp.minimum(i + 1, n_tiles - 1)
    mcur = lambda kk: pl.BlockSpec((None, None, 1, D), lambda i: (layer, cur(i) // n_l, 0, kk))
    mprv = lambda kk: pl.BlockSpec((None, None, 1, D), lambda i: (layer, prv(i) // n_l, 0, kk))
    mnxt = lambda kk: pl.BlockSpec((None, None, 1, D), lambda i: (layer, nxt(i) // n_l, 0, kk))
    once = lambda shape: pl.BlockSpec((None,) + shape, lambda i: (layer,) + (0,) * len(shape),
                                      pipeline_mode=pl.Buffered(1))
    return pl.pallas_call(
        functools.partial(_layerp_body, layer, n_l, n_tiles),
        grid=(n_tiles + 1,),
        in_specs=[
            pl.BlockSpec((None, TL, D), lambda i: (cur(i) // n_l, cur(i) % n_l, 0)),
            pl.BlockSpec((None, TL, D), lambda i: (nxt(i) // n_l, nxt(i) % n_l, 0)),
            mcur(0), mcur(1), mcur(2), mnxt(0), mnxt(1), mcur(3), mcur(4), mprv(5),
            _full((1, D)), _full((1, D)), _full((1, D)), _full((1, D)),
            once((D, N_IN)), once((D, D)), once((D, D_FF)), once((D_FF, D)),
            _full((1, W_A)), _full((1, W_A)), _full((CHUNK_A, HEADS_A * CHUNK_A)), _full((CHUNK_A, W_A)),
            _full(blb.shape), _full((1, W_B)),
            _full((TAPS, W_C)), _full((1, W_C)), _full((1, W_C)), _full((1, W_C)),
            _full(masks.shape), _full(lmask.shape), _full(bdm.shape), _full(gavg.shape),
        ],
        out_specs=[
            pl.BlockSpec((None, TL, D), lambda i: (prv(i) // n_l, prv(i) % n_l, 0)),
            pl.BlockSpec((None, HEADS_B, DK, DV), lambda i: (cur(i) // n_l, 0, 0, 0)),
            pl.BlockSpec((None, TAPS - 1, W_C), lambda i: (cur(i) // n_l, 0, 0)),
        ],
        out_shape=[
            jax.ShapeDtypeStruct((nb, L, D), F32),
            jax.ShapeDtypeStruct((nb, HEADS_B, DK, DV), F32),
            jax.ShapeDtypeStruct((nb, TAPS - 1, W_C), F32),
        ],
        scratch_shapes=[
            pltpu.VMEM((TL, N_IN), F32),
            pltpu.VMEM((TAIL + TL, W_C), F32),
            pltpu.VMEM((7, TAIL + TL - 8, W_C), F32),
            pltpu.VMEM((TL, D), BF16),
            pltpu.VMEM((2, TL, D), F32),
            pltpu.VMEM((TL, D), BF16),
            pltpu.VMEM((TL, D), BF16),
            pltpu.VMEM((HEADS_B // 2, 128, 128), F32),
            pltpu.VMEM((TAIL, W_C), F32),
        ],
        compiler_params=pltpu.CompilerParams(
            dimension_semantics=("arbitrary",), vmem_limit_bytes=VMEM_LIMIT),
        name=f"layer_prompt_l{layer}",
    )(x, x, modp, modp, modp, modp, modp, modp, modp, modp, row(gpre), row(gpost), row(gpre2), row(gpost2),
      win_b, wout_b, wup_b, wdn_b, row(alg), row(alb), wcat, abias,
      blb, row(gng), cw, row(cb), row(clg), row(clb), masks, lmask, bdm, gavg)


def _mixs_body(layer, nt,
               x_ref, sh_ref, sc_ref, gt_ref, gpre_ref, gpost_ref, win_ref, wout_ref,
               alg_ref, alb_ref, acoef_ref, abias_ref, blb_ref, gng_ref,
               cw_ref, cb_ref, clg_ref, clb_ref, gavg_ref, cst_ref, s_ref,
               o_ref, so_ref, xg_ref, v_ref,
               z_ref, y_ref, ft_ref, kt_ref, qt_ref, vt_ref, ot_ref):
    hd = pl.program_id(0)
    ns = x_ref.shape[1]

    @pl.when(hd == 0)
    def _():
        x = x_ref[...]
        h = _rms(x, gpre_ref[...]) * (1.0 + sc_ref[...]) + sh_ref[...]
        z_ref[...] = _dot(h.reshape(nt * ns, D).astype(BF16), win_ref[...])

        za = _gelu(z_ref[:, OFF_A:OFF_Q])
        v = _ln(za[:, W_A:], alg_ref[...], alb_ref[...])
        for t in range(nt):
            mixed = abias_ref[pl.ds(t, 1), :]
            for s in range(t + 1):
                mixed = mixed + acoef_ref[pl.ds(t * nt + s, 1), :] * v[s * ns:(s + 1) * ns, :]
            y_ref[t * ns:(t + 1) * ns, 0:W_A] = (za[t * ns:(t + 1) * ns, :W_A] * mixed).astype(BF16)
            v_ref[t] = v[t * ns:(t + 1) * ns, :]

        xg = z_ref[:, OFF_C:OFF_C + W_C] * jax.nn.sigmoid(z_ref[:, OFF_C + W_C:N_IN])
        hist = TAPS - 1
        for t in range(nt):
            xg_ref[t] = xg[t * ns:(t + 1) * ns, :]
        for t in range(nt):
            acc = jnp.zeros((ns, W_C), F32)
            for tap in range(TAPS):
                pos = t + tap
                src = cst_ref[pos] if pos < hist else xg[(pos - hist) * ns:(pos - hist + 1) * ns, :]
                acc = acc + src * cw_ref[pl.ds(tap, 1), :]
            yc = _silu(_ln(acc + cb_ref[...], clg_ref[...], clb_ref[...]))
            y_ref[t * ns:(t + 1) * ns, W_A + W_B:D] = yc.astype(BF16)

        lb = _lower_bound(blb_ref[...], layer)
        for t in range(nt):
            r = slice(t * ns, (t + 1) * ns)
            f = lb + (1.0 - lb) * jax.nn.sigmoid(z_ref[r, OFF_F:OFF_I])
            ft_ref[t] = f.T
            kt_ref[t] = (1.0 - f).T
            qt_ref[t] = (_silu(z_ref[r, OFF_Q:OFF_F]) * (DK ** -0.5)).T
            vt_ref[t] = z_ref[r, OFF_I:OFF_G].T

    base = pl.multiple_of(hd * DK, DK)
    vts = [vt_ref[t, pl.ds(base, DV), :] for t in range(nt)]

    def krow(i, accs):
        srow = s_ref[i]
        new = []
        for t in range(nt):
            srow = ft_ref[t, pl.ds(base + i, 1), :] * srow + kt_ref[t, pl.ds(base + i, 1), :] * vts[t]
            new.append(accs[t] + qt_ref[t, pl.ds(base + i, 1), :] * srow)
        so_ref[i] = srow
        return tuple(new)

    accs = lax.fori_loop(0, DK, krow, tuple(jnp.zeros((DV, ns), F32) for _ in range(nt)))
    for t in range(nt):
        ot_ref[t, pl.ds(base, DV), :] = accs[t]

    @pl.when(hd == HEADS_B - 1)
    def _():
        gavg = gavg_ref[...]
        for t in range(nt):
            r = slice(t * ns, (t + 1) * ns)
            o = ot_ref[t].T
            yb = _group_rms(o, gavg, gng_ref[...]) * _silu(z_ref[r, OFF_G:OFF_C])
            y_ref[r, W_A:W_A + W_B] = yb.astype(BF16)
        y = _dot(y_ref[...], wout_ref[...]).reshape(nt, ns, D)
        o_ref[...] = x_ref[...] + gt_ref[...] * _rms(y, gpost_ref[...])


def _mixs_call(layer, xs, mods, gpre, gpost, win_b, wout_b, alg, alb, acoef, abias, blb, gng,
               cw, cb, clg, clb, gavg, cst, s_in):
    nt, ns, _ = xs.shape
    row = lambda a: a.reshape(1, -1)
    mspec = lambda kk: pl.BlockSpec((None, ns, D), lambda h: (layer, 0, kk))
    hist = TAPS - 1
    return pl.pallas_call(
        functools.partial(_mixs_body, layer, nt),
        grid=(HEADS_B,),
        in_specs=[
            _full((nt, ns, D)),
            mspec(0), mspec(1), mspec(2),
            _full((1, D)), _full((1, D)),
            pl.BlockSpec((None, D, N_IN), lambda h: (layer, 0, 0)), pl.BlockSpec((None, D, D), lambda h: (layer, 0, 0)),
            _full((1, W_A)), _full((1, W_A)), _full(acoef.shape), _full(abias.shape),
            _full(blb.shape), _full((1, W_B)),
            _full((TAPS, W_C)), _full((1, W_C)), _full((1, W_C)), _full((1, W_C)),
            _full(gavg.shape), _full((hist, ns, W_C)),
            pl.BlockSpec((None, DK, DV, ns), lambda h: (h, 0, 0, 0)),
        ],
        out_specs=[
            _full((nt, ns, D)),
            pl.BlockSpec((None, DK, DV, ns), lambda h: (h, 0, 0, 0)),
            _full((nt, ns, W_C)),
            _full((nt, ns, W_A)),
        ],
        out_shape=[
            jax.ShapeDtypeStruct((nt, ns, D), F32),
            jax.ShapeDtypeStruct((HEADS_B, DK, DV, ns), F32),
            jax.ShapeDtypeStruct((nt, ns, W_C), F32),
            jax.ShapeDtypeStruct((nt, ns, W_A), F32),
        ],
        scratch_shapes=[
            pltpu.VMEM((nt * ns, N_IN), F32),
            pltpu.VMEM((nt * ns, D), BF16),
            pltpu.VMEM((nt, W_B, ns), F32),
            pltpu.VMEM((nt, W_B, ns), F32),
            pltpu.VMEM((nt, W_B, ns), F32),
            pltpu.VMEM((nt, W_B, ns), F32),
            pltpu.VMEM((nt, W_B, ns), F32),
        ],
        compiler_params=pltpu.CompilerParams(
            dimension_semantics=("arbitrary",), vmem_limit_bytes=VMEM_LIMIT),
        name=f"mix_sample_l{layer}",
    )(xs, mods, mods, mods, row(gpre), row(gpost), win_b, wout_b, row(alg), row(alb), acoef, abias,
      blb, row(gng), cw, row(cb), row(clg), row(clb), gavg, cst, s_in)


def _mlps_body(x_ref, sh_ref, sc_ref, gt_ref, gpre_ref, gpost_ref, wup_ref, wdn_ref, o_ref, h_ref, acc_ref):
    c = pl.program_id(0)
    nt, ns, _ = x_ref.shape

    @pl.when(c == 0)
    def _():
        h = _rms(x_ref[...], gpre_ref[...] * (1.0 + sc_ref[...])) + sh_ref[...]
        h_ref[...] = h.reshape(nt * ns, D).astype(BF16)
        acc_ref[...] = jnp.zeros_like(acc_ref)

    u = jnp.maximum(_dot(h_ref[...], wup_ref[...]), 0.0)
    acc_ref[...] += _dot((u * u).astype(BF16), wdn_ref[...])

    @pl.when(c == pl.num_programs(0) - 1)
    def _():
        o_ref[...] = x_ref[...] + _rms(acc_ref[...].reshape(nt, ns, D), gt_ref[...] * gpost_ref[...])


def _mlps_call(layer, x, mods, gpre, gpost, wup_b, wdn_b):
    nt, ns, _ = x.shape
    row = lambda v: v.reshape(1, -1)
    cw = D_FF // 4
    mspec = lambda kk: pl.BlockSpec((None, ns, D), lambda c: (layer, 0, kk))
    return pl.pallas_call(
        _mlps_body,
        grid=(D_FF // cw,),
        in_specs=[_full((nt, ns, D)), mspec(3), mspec(4), mspec(5), _full((1, D)), _full((1, D)),
                  pl.BlockSpec((None, D, cw), lambda c: (layer, 0, c)),
                  pl.BlockSpec((None, cw, D), lambda c: (layer, c, 0))],
        out_specs=_full((nt, ns, D)),
        out_shape=jax.ShapeDtypeStruct(x.shape, F32),
        scratch_shapes=[pltpu.VMEM((nt * ns, D), BF16), pltpu.VMEM((nt * ns, D), F32)],
        compiler_params=pltpu.CompilerParams(
            dimension_semantics=("arbitrary",), vmem_limit_bytes=VMEM_LIMIT),
        name=f"mlp_sample_l{layer}",
    )(x, mods, mods, mods, row(gpre), row(gpost), wup_b, wdn_b)


def kernel(x_prompt, x_sample, state_hgrn, state_conv, c_prompt, c_sample, w_ada, b_ada, g_pre_mix, g_post_mix, g_pre_mlp, g_post_mlp, w_in, a_ln_g, a_ln_b, a_w_s, a_b_s, b_lb, b_gn_g, c_w_dw, c_b_dw, c_ln_g, c_ln_b, w_out, w_up, w_down):
    depth = w_in.shape[0]
    nb = x_prompt.shape[0]
    ns, nt, _ = x_sample.shape
    mod_s, mod_p = _mod_call(c_sample, c_prompt, w_ada, b_ada)
    modp = mod_p.reshape(depth, nb, 1, 6 * D)
    masks = jnp.asarray(_level_masks(), dtype=BF16)
    lmask = jnp.asarray(_first_head_lanes(), dtype=BF16)
    bdm = jnp.asarray(_block_diag_mask())
    gavg = jnp.asarray(_group_avg(), dtype=BF16)
    win_b, wout_b, wup_b, wdn_b = (w.astype(BF16) for w in (w_in, w_out, w_up, w_down))

    xp = x_prompt
    xs = jnp.transpose(x_sample, (1, 0, 2))
    hg_p, hg_s, cv_p, cv_s, v_s = [], [], [], [], []
    for l in range(depth):
        wcat = jnp.transpose(a_w_s[l], (1, 0, 2)).reshape(CHUNK_A, HEADS_A * CHUNK_A)
        abias = jnp.repeat(a_b_s[l].T, DH_A, axis=1)
        acoef = jnp.repeat(jnp.transpose(a_w_s[l][:, :nt, :nt], (1, 2, 0)).reshape(nt * nt, HEADS_A),
                           DH_A, axis=1)

        xp, hs_l, cv_l = _layerp_call(
            l, xp, modp, g_pre_mix[l], g_post_mix[l], g_pre_mlp[l], g_post_mlp[l], win_b, wout_b, wup_b, wdn_b,
            a_ln_g[l], a_ln_b[l], wcat, abias, b_lb, b_gn_g[l], c_w_dw[l], c_b_dw[l], c_ln_g[l], c_ln_b[l],
            masks, lmask, bdm, gavg)

        cst = jnp.transpose(state_conv[l], (1, 0, 2))
        s_in = jnp.transpose(state_hgrn[l], (1, 2, 3, 0))
        xs, s_out, xg_s, vrow = _mixs_call(
            l, xs, mod_s, g_pre_mix[l], g_post_mix[l], win_b, wout_b, a_ln_g[l], a_ln_b[l], acoef,
            abias[:nt], b_lb, b_gn_g[l], c_w_dw[l], c_b_dw[l], c_ln_g[l], c_ln_b[l], gavg, cst, s_in)
        xs = _mlps_call(l, xs, mod_s, g_pre_mlp[l], g_post_mlp[l], wup_b, wdn_b)

        hg_p.append(hs_l)
        hg_s.append(jnp.transpose(s_out, (3, 0, 1, 2)))
        cv_p.append(cv_l)
        cv_s.append(jnp.concatenate([state_conv[l][:, nt:, :], jnp.transpose(xg_s, (1, 0, 2))], axis=1))
        v_s.append(jnp.transpose(vrow, (1, 0, 2)))

    return (xp, jnp.transpose(xs, (1, 0, 2)), jnp.stack(hg_p), jnp.stack(hg_s),
            jnp.stack(cv_p), jnp.stack(cv_s), jnp.stack(v_s))
```

```python
import functools

import numpy as np
import jax
import jax.numpy as jnp
from jax import lax
from jax.experimental import pallas as pl
from jax.experimental.pallas import tpu as pltpu

F32 = jnp.float32
BF16 = jnp.bfloat16

D = 1024
HEADS_A, DH_A, W_A, CHUNK_A = 4, 64, 256, 128
HEADS_B, DK, DV, W_B = 8, 64, 64, 512
SUB = 16
TILE_B = 128
W_C, TAPS = 256, 31
TAIL = 32
D_FF = 4096
N_IN = 3072
EPS = 1e-6
OFF_A, OFF_Q, OFF_F, OFF_I, OFF_G, OFF_C = 0, 512, 1024, 1536, 2048, 2560

TL = 512
MLP_CHUNKS = 8
VMEM_LIMIT = 60 * 1024 * 1024


def _dot(a, b):
    return jnp.dot(a, b, preferred_element_type=F32)


def _dot_nt(a, b):
    return lax.dot_general(a, b, (((1,), (1,)), ((), ())), preferred_element_type=F32)


def _rms(x, g):
    return x * lax.rsqrt(jnp.mean(x * x, axis=-1, keepdims=True) + EPS) * g


def _ln(x, g, b):
    xc = x - jnp.mean(x, axis=-1, keepdims=True)
    return xc * lax.rsqrt(jnp.mean(xc * xc, axis=-1, keepdims=True) + EPS) * g + b


def _silu(x):
    return x * jax.nn.sigmoid(x)


def _gelu(x):
    return 0.5 * x * (1.0 + lax.erf(x * 0.7071067811865476))


def _lower_bound(blb, layer):
    m = jnp.max(blb, axis=0, keepdims=True)
    e = jnp.exp(blb - m)
    tot = jnp.sum(e, axis=0, keepdims=True)
    acc = jnp.zeros_like(tot)
    for i in range(1, layer + 1):
        acc = acc + e[i:i + 1, :]
    return acc / tot


def _group_rms(o, gavg, g):
    o2 = o * o
    hi = o2.astype(BF16)
    lo = (o2 - hi.astype(F32)).astype(BF16)
    ms = _dot(hi, gavg) + _dot(lo, gavg)
    return o * lax.rsqrt(ms + EPS) * g


def _mod_body(cs_ref, cp_ref, w_ref, b_ref, os_ref, op_ref):
    w = w_ref[...].astype(BF16)
    os_ref[...] = _dot(_silu(cs_ref[...]).astype(BF16), w) + b_ref[...]
    op_ref[...] = _dot(_silu(cp_ref[...]).astype(BF16), w) + b_ref[...]


def _mod_call(c_s, c_p, w_ada, b_ada):
    depth, _, n6 = w_ada.shape
    ns, nb = c_s.shape[0], c_p.shape[0]
    bn = 1536
    return pl.pallas_call(
        _mod_body,
        grid=(depth, n6 // bn),
        in_specs=[
            pl.BlockSpec((ns, D), lambda l, j: (0, 0)),
            pl.BlockSpec((nb, D), lambda l, j: (0, 0)),
            pl.BlockSpec((None, D, bn), lambda l, j: (l, 0, j)),
            pl.BlockSpec((None, 1, bn), lambda l, j: (l, 0, j)),
        ],
        out_specs=[pl.BlockSpec((None, ns, bn), lambda l, j: (l, 0, j)),
                   pl.BlockSpec((None, nb, bn), lambda l, j: (l, 0, j))],
        out_shape=[jax.ShapeDtypeStruct((depth, ns, n6), F32), jax.ShapeDtypeStruct((depth, nb, n6), F32)],
        compiler_params=pltpu.CompilerParams(
            dimension_semantics=("arbitrary", "arbitrary"), vmem_limit_bytes=VMEM_LIMIT),
        name="adaln_mod",
    )(c_s, c_p, w_ada, b_ada.reshape(depth, 1, n6))


def _gmlp_chunk(za, ln_g, ln_b, wcat_b, bias):
    z = _gelu(za)
    u = z[:, :W_A]
    v = _ln(z[:, W_A:], ln_g, ln_b)
    lane_h = lax.broadcasted_iota(jnp.int32, (CHUNK_A, W_A), 1) // DH_A
    vbd = jnp.concatenate(
        [jnp.where(lane_h == h, v, 0.0).astype(BF16) for h in range(HEADS_A)], axis=0)
    mixed = _dot(wcat_b, vbd) + bias
    return u * mixed


def _hgrn_tile(zq, zf, zi, zg, lb, gn_g, st_ref, masks_ref, lmask_ref, bd_ref, gavg, keep, fill):
    nch = TILE_B // SUB
    q = _silu(zq) * (DK ** -0.5)
    f = lb + (1.0 - lb) * jax.nn.sigmoid(zf)
    lf = jnp.log(f)
    k = 1.0 - f

    r15 = lax.broadcasted_iota(jnp.int32, (TILE_B, W_B), 0) & (SUB - 1)
    cs = lf
    for d in (1, 2, 4, 8):
        cs = cs + jnp.where(r15 >= d, pltpu.roll(cs, d, 0), 0.0)
    tot = [cs[SUB * c + SUB - 1:SUB * c + SUB, :] for c in range(nch)]
    tot_b = jnp.concatenate([jnp.broadcast_to(t, (SUB, W_B)) for t in tot], axis=0)
    q_in = q * jnp.exp(cs)
    k_out = k * jnp.exp(tot_b - cs)

    q0, k0 = [], []
    for c in range(nch):
        r = slice(SUB * c, SUB * c + SUB)
        if c % 2 == 0:
            q0.append(q[r] * jnp.exp(cs[r] - tot[c]))
            k0.append(k_out[r])
        else:
            q0.append(q_in[r])
            k0.append(k[r] * jnp.exp(-cs[r]))
    q0 = jnp.concatenate(q0, axis=0)
    k0 = jnp.concatenate(k0, axis=0)

    pre = [jnp.zeros((1, W_B), F32)]
    for c in range(nch):
        pre.append(pre[-1] + tot[c])

    def rows(fn):
        parts = []
        for c in range(nch):
            e = fn(c)
            parts.append(jnp.zeros((SUB, W_B), F32) if e is None
                         else jnp.broadcast_to(jnp.exp(e), (SUB, W_B)))
        return jnp.concatenate(parts, axis=0)

    a_st = rows(lambda c: pre[c])
    b_st = rows(lambda c: pre[nch] - pre[c + 1])
    a_64 = rows(lambda c: pre[c] - pre[4] if c >= 4 else None)
    b_64 = rows(lambda c: pre[4] - pre[c + 1] if c < 4 else None)
    a_32 = rows(lambda c: pre[c] - pre[c & ~3 | 2] if (c & 2) else None)
    b_32 = rows(lambda c: pre[c & ~3 | 2] - pre[c + 1] if not (c & 2) else None)
    decay = jnp.exp(pre[nch])

    m0 = lmask_ref[...] > 0.5
    zb = jnp.zeros((TILE_B, 128), BF16)

    def stack2(x):
        xb = x.astype(BF16)
        return jnp.concatenate([jnp.where(m0, xb, zb), jnp.where(m0, zb, xb)], axis=0)

    levels = ((q0, k0), (q_in * a_32, k_out * b_32), (q_in * a_64, k_out * b_64))
    q_st = q_in * a_st
    k_st = k_out * b_st
    v_t = zi.T
    bd = jnp.logical_and(bd_ref[...] > 0.5, keep)
    decay = jnp.where(keep, decay, 1.0)
    mk = [masks_ref[lv] > 0.5 for lv in range(len(levels))]

    fill()
    atts = []
    for p in range(HEADS_B // 2):
        sl = slice(128 * p, 128 * p + 128)
        att = jnp.zeros((TILE_B, 2 * TILE_B), BF16)
        for lv, (qa, ka) in reversed(list(enumerate(levels))):
            s = _dot_nt(qa[:, sl].astype(BF16), stack2(ka[:, sl]))
            att = jnp.where(mk[lv], s.astype(BF16), att)
        atts.append(att)
    fill()
    outs = []
    for p in range(HEADS_B // 2):
        sl = slice(128 * p, 128 * p + 128)
        o = _dot(atts[p], stack2(zi[:, sl]))
        st = st_ref[p]
        o = o + _dot_nt(q_st[:, sl].astype(BF16), st.astype(BF16))
        upd = _dot(v_t[sl, :].astype(BF16), k_st[:, sl].astype(BF16))
        st_ref[p] = st * decay[:, sl] + jnp.where(bd, upd, 0.0)
        outs.append(o)
    o = jnp.concatenate(outs, axis=1)
    fill()
    return _group_rms(o, gavg, gn_g) * _silu(zg)


def _layerp_body(layer, n_l, n_tiles,
                 x_ref, xn_ref, sh_ref, sc_ref, gt_ref, shn_ref, scn_ref, sh2_ref, sc2_ref, gt2_ref,
                 gpre_ref, gpost_ref, gpre2_ref, gpost2_ref, win_ref, wout_ref, wup_ref, wdn_ref,
                 alg_ref, alb_ref, wcat_ref, abias_ref, blb_ref, gng_ref,
                 cw_ref, cb_ref, clg_ref, clb_ref, masks_ref, lmask_ref, bd_ref, gavg_ref,
                 o_ref, hs_ref, cv_ref,
                 z_ref, xx_ref, xs_ref, y_ref, x1_ref, hm_ref, h_ref, st_ref, tail_ref):
    i = pl.program_id(0)
    keep = i < n_tiles

    @pl.when(jnp.logical_and(i % n_l == 0, keep))
    def _():
        st_ref[...] = jnp.zeros_like(st_ref)
        tail_ref[...] = jnp.zeros_like(tail_ref)

    @pl.when(i == 0)
    def _():
        x1_ref[1] = jnp.zeros((TL, D), F32)
        hm_ref[0] = jnp.zeros((TL, D), BF16)
        h_ref[0] = (_rms(x_ref[...], gpre_ref[...] * (1.0 + sc_ref[...])) + sh_ref[...]).astype(BF16)

    cwid = D_FF // MLP_CHUNKS
    slot, nslot = i % 2, (i + 1) % 2
    half = TL // 2
    rows = (slice(0, half), slice(half, TL))
    ym, pend = [None, None], {}

    def up(c, rh):
        u = jnp.maximum(_dot(hm_ref[slot, rows[rh], :], wup_ref[:, c * cwid:(c + 1) * cwid]), 0.0)
        pend[c, rh] = (u * u).astype(BF16)

    def down(c, rh):
        part = _dot(pend.pop((c, rh)), wdn_ref[c * cwid:(c + 1) * cwid, :])
        ym[rh] = part if ym[rh] is None else ym[rh] + part

    def close(rh):
        o_ref[rows[rh], :] = x1_ref[nslot, rows[rh], :] + _rms(ym[rh], gt2_ref[...] * gpost2_ref[...])

    units = [functools.partial(up, 0, 0), functools.partial(up, 0, 1)]
    for c in range(1, MLP_CHUNKS):
        for rh in range(2):
            units += [functools.partial(up, c, rh), functools.partial(down, c - 1, rh)]
    units += [functools.partial(down, MLP_CHUNKS - 1, 0), functools.partial(close, 0),
              functools.partial(down, MLP_CHUNKS - 1, 1), functools.partial(close, 1)]

    def fill(n=1):
        for _ in range(n):
            if units:
                units.pop(0)()

    z_ref[:, OFF_C:N_IN] = _dot(h_ref[slot], win_ref[:, OFF_C:N_IN])
    z_ref[:, 0:OFF_C] = _dot(h_ref[slot], win_ref[:, 0:OFF_C])
    h_ref[nslot] = (_rms(xn_ref[...], gpre_ref[...] * (1.0 + scn_ref[...])) + shn_ref[...]).astype(BF16)

    fill(2)

    xg = z_ref[:, OFF_C:OFF_C + W_C] * jax.nn.sigmoid(z_ref[:, OFF_C + W_C:N_IN])
    old_tail = tail_ref[...]
    xx_ref[0:TAIL, :] = old_tail
    xx_ref[TAIL:TAIL + TL, :] = xg
    tail_ref[...] = jnp.where(keep, xg[TL - TAIL:TL, :], old_tail)
    fill()
    span = TL + TAIL - 8
    for r in range(1, 8):
        xs_ref[r - 1] = xx_ref[pl.ds(r, span), :]
    fill()
    rb = 64
    for c in range(TL // rb):
        acc = jnp.zeros((rb, W_C), F32)
        for t in range(TAPS):
            a, r = divmod(TAIL - (TAPS - 1) + t, 8)
            src = xx_ref[pl.ds(c * rb + 8 * a, rb), :] if r == 0 else xs_ref[r - 1, pl.ds(c * rb + 8 * a, rb), :]
            acc = acc + src * cw_ref[pl.ds(t, 1), :]
        yc = _silu(_ln(acc + cb_ref[...], clg_ref[...], clb_ref[...]))
        y_ref[c * rb:(c + 1) * rb, W_A + W_B:D] = yc.astype(BF16)
        fill()

    lb = _lower_bound(blb_ref[...], layer)
    gavg = gavg_ref[...]
    for t in range(TL // TILE_B):
        r = slice(t * TILE_B, (t + 1) * TILE_B)
        yb = _hgrn_tile(z_ref[r, OFF_Q:OFF_F], z_ref[r, OFF_F:OFF_I], z_ref[r, OFF_I:OFF_G],
                        z_ref[r, OFF_G:OFF_C], lb, gng_ref[...], st_ref, masks_ref, lmask_ref, bd_ref, gavg,
                        keep, fill)
        y_ref[r, W_A:W_A + W_B] = yb.astype(BF16)

    tri = (lax.broadcasted_iota(jnp.int32, (CHUNK_A, HEADS_A * CHUNK_A), 1) & (CHUNK_A - 1)
           ) <= lax.broadcasted_iota(jnp.int32, (CHUNK_A, HEADS_A * CHUNK_A), 0)
    wcat_b = jnp.where(tri, wcat_ref[...], 0.0).astype(BF16)
    for c in range(TL // CHUNK_A):
        r = slice(c * CHUNK_A, (c + 1) * CHUNK_A)
        ya = _gmlp_chunk(z_ref[r, OFF_A:OFF_Q], alg_ref[...], alb_ref[...], wcat_b, abias_ref[...])
        y_ref[r, 0:W_A] = ya.astype(BF16)
        fill()

    fill(len(units) - 8)

    for rh in range(2):
        y = _dot(y_ref[rows[rh], :], wout_ref[...])
        fill(2)
        x1 = x_ref[rows[rh], :] + _rms(y, gt_ref[...] * gpost_ref[...])
        x1_ref[slot, rows[rh], :] = x1
        hm_ref[nslot, rows[rh], :] = (_rms(x1, gpre2_ref[...] * (1.0 + sc2_ref[...])) + sh2_ref[...]).astype(BF16)
    fill(len(units))

    @pl.when(jnp.logical_and(i % n_l == n_l - 1, keep))
    def _():
        for p in range(HEADS_B // 2):
            t = st_ref[p].T
            hs_ref[2 * p] = t[0:DK, 0:DV]
            hs_ref[2 * p + 1] = t[DK:2 * DK, DV:2 * DV]
        cv_ref[...] = tail_ref[TAIL - (TAPS - 1):TAIL, :]


def _level_masks():
    t = np.arange(TILE_B)[:, None]
    s = np.arange(2 * TILE_B)[None, :] % TILE_B
    ct, cs = t // SUB, s // SUB
    m = [
        (ct // 2 == cs // 2) & (s <= t),
        (ct // 2 == cs // 2 + 1) & ((ct // 2) % 2 == 1),
        (ct // 4 == cs // 4 + 1) & ((ct // 4) % 2 == 1),
    ]
    return np.stack(m).astype(np.float32)


def _first_head_lanes():
    return np.broadcast_to(np.arange(128)[None, :] < DK, (TILE_B, 128)).astype(np.float32)


def _block_diag_mask():
    i = np.arange(128)
    return ((i[:, None] // DK) == (i[None, :] // DK)).astype(np.float32)


def _group_avg():
    i = np.arange(W_B)
    return ((i[:, None] // DV) == (i[None, :] // DV)).astype(np.float32) / DV


def _full(shape):
    nd = len(shape)
    return pl.BlockSpec(shape, lambda *_: (0,) * nd)


def _layerp_call(layer, x, modp, gpre, gpost, gpre2, gpost2, win_b, wout_b, wup_b, wdn_b,
                 alg, alb, wcat, abias, blb, gng, cw, cb, clg, clb, masks, lmask, bdm, gavg):
    nb, L, _ = x.shape
    n_l = L // TL
    n_tiles = nb * n_l
    row = lambda a: a.reshape(1, -1)
    cur = lambda i: jnp.minimum(i, n_tiles - 1)
    prv = lambda i: jnp.maximum(i - 1, 0)
    nxt = lambda i: jnp.minimum(i + 1, n_tiles - 1)
    mcur = lambda kk: pl.BlockSpec((None, None, 1, D), lambda i: (layer, cur(i) // n_l, 0, kk))
    mprv = lambda kk: pl.BlockSpec((None, None, 1, D), lambda i: (layer, prv(i) // n_l, 0, kk))
    mnxt = lambda kk: pl.BlockSpec((None, None, 1, D), lambda i: (layer, nxt(i) // n_l, 0, kk))
    once = lambda shape: pl.BlockSpec((None,) + shape, lambda i: (layer,) + (0,) * len(shape),
                                      pipeline_mode=pl.Buffered(1))
    return pl.pallas_call(
        functools.partial(_layerp_body, layer, n_l, n_tiles),
        grid=(n_tiles + 1,),
        in_specs=[
            pl.BlockSpec((None, TL, D), lambda i: (cur(i) // n_l, cur(i) % n_l, 0)),
            pl.BlockSpec((None, TL, D), lambda i: (nxt(i) // n_l, nxt(i) % n_l, 0)),
            mcur(0), mcur(1), mcur(2), mnxt(0), mnxt(1), mcur(3), mcur(4), mprv(5),
            _full((1, D)), _full((1, D)), _full((1, D)), _full((1, D)),
            once((D, N_IN)), once((D, D)), once((D, D_FF)), once((D_FF, D)),
            _full((1, W_A)), _full((1, W_A)), _full((CHUNK_A, HEADS_A * CHUNK_A)), _full((CHUNK_A, W_A)),
            _full(blb.shape), _full((1, W_B)),
            _full((TAPS, W_C)), _full((1, W_C)), _full((1, W_C)), _full((1, W_C)),
            _full(masks.shape), _full(lmask.shape), _full(bdm.shape), _full(gavg.shape),
        ],
        out_specs=[
            pl.BlockSpec((None, TL, D), lambda i: (prv(i) // n_l, prv(i) % n_l, 0)),
            pl.BlockSpec((None, HEADS_B, DK, DV), lambda i: (cur(i) // n_l, 0, 0, 0)),
            pl.BlockSpec((None, TAPS - 1, W_C), lambda i: (cur(i) // n_l, 0, 0)),
        ],
        out_shape=[
            jax.ShapeDtypeStruct((nb, L, D), F32),
            jax.ShapeDtypeStruct((nb, HEADS_B, DK, DV), F32),
            jax.ShapeDtypeStruct((nb, TAPS - 1, W_C), F32),
        ],
        scratch_shapes=[
            pltpu.VMEM((TL, N_IN), F32),
            pltpu.VMEM((TAIL + TL, W_C), F32),
            pltpu.VMEM((7, TAIL + TL - 8, W_C), F32),
            pltpu.VMEM((TL, D), BF16),
            pltpu.VMEM((2, TL, D), F32),
            pltpu.VMEM((2, TL, D), BF16),
            pltpu.VMEM((2, TL, D), BF16),
            pltpu.VMEM((HEADS_B // 2, 128, 128), F32),
            pltpu.VMEM((TAIL, W_C), F32),
        ],
        compiler_params=pltpu.CompilerParams(
            dimension_semantics=("arbitrary",), vmem_limit_bytes=VMEM_LIMIT),
        name=f"layer_prompt_l{layer}",
    )(x, x, modp, modp, modp, modp, modp, modp, modp, modp, row(gpre), row(gpost), row(gpre2), row(gpost2),
      win_b, wout_b, wup_b, wdn_b, row(alg), row(alb), wcat, abias,
      blb, row(gng), cw, row(cb), row(clg), row(clb), masks, lmask, bdm, gavg)


def _mixs_body(layer, nt,
               x_ref, sh_ref, sc_ref, gt_ref, gpre_ref, gpost_ref, win_ref, wout_ref,
               alg_ref, alb_ref, acoef_ref, abias_ref, blb_ref, gng_ref,
               cw_ref, cb_ref, clg_ref, clb_ref, gavg_ref, cst_ref, s_ref,
               o_ref, so_ref, xg_ref, v_ref,
               z_ref, y_ref, ft_ref, kt_ref, qt_ref, vt_ref, ot_ref):
    hd = pl.program_id(0)
    ns = x_ref.shape[1]

    @pl.when(hd == 0)
    def _():
        x = x_ref[...]
        h = _rms(x, gpre_ref[...]) * (1.0 + sc_ref[...]) + sh_ref[...]
        z_ref[...] = _dot(h.reshape(nt * ns, D).astype(BF16), win_ref[...])

        za = _gelu(z_ref[:, OFF_A:OFF_Q])
        v = _ln(za[:, W_A:], alg_ref[...], alb_ref[...])
        for t in range(nt):
            mixed = abias_ref[pl.ds(t, 1), :]
            for s in range(t + 1):
                mixed = mixed + acoef_ref[pl.ds(t * nt + s, 1), :] * v[s * ns:(s + 1) * ns, :]
            y_ref[t * ns:(t + 1) * ns, 0:W_A] = (za[t * ns:(t + 1) * ns, :W_A] * mixed).astype(BF16)
            v_ref[t] = v[t * ns:(t + 1) * ns, :]

        xg = z_ref[:, OFF_C:OFF_C + W_C] * jax.nn.sigmoid(z_ref[:, OFF_C + W_C:N_IN])
        hist = TAPS - 1
        for t in range(nt):
            xg_ref[t] = xg[t * ns:(t + 1) * ns, :]
        for t in range(nt):
            acc = jnp.zeros((ns, W_C), F32)
            for tap in range(TAPS):
                pos = t + tap
                src = cst_ref[pos] if pos < hist else xg[(pos - hist) * ns:(pos - hist + 1) * ns, :]
                acc = acc + src * cw_ref[pl.ds(tap, 1), :]
            yc = _silu(_ln(acc + cb_ref[...], clg_ref[...], clb_ref[...]))
            y_ref[t * ns:(t + 1) * ns, W_A + W_B:D] = yc.astype(BF16)

        lb = _lower_bound(blb_ref[...], layer)
        for t in range(nt):
            r = slice(t * ns, (t + 1) * ns)
            f = lb + (1.0 - lb) * jax.nn.sigmoid(z_ref[r, OFF_F:OFF_I])
            ft_ref[t] = f.T
            kt_ref[t] = (1.0 - f).T
            qt_ref[t] = (_silu(z_ref[r, OFF_Q:OFF_F]) * (DK ** -0.5)).T
            vt_ref[t] = z_ref[r, OFF_I:OFF_G].T

    base = pl.multiple_of(hd * DK, DK)
    vts = [vt_ref[t, pl.ds(base, DV), :] for t in range(nt)]

    def krow(i, accs):
        srow = s_ref[i]
        new = []
        for t in range(nt):
            srow = ft_ref[t, pl.ds(base + i, 1), :] * srow + kt_ref[t, pl.ds(base + i, 1), :] * vts[t]
            new.append(accs[t] + qt_ref[t, pl.ds(base + i, 1), :] * srow)
        so_ref[i] = srow
        return tuple(new)

    accs = lax.fori_loop(0, DK, krow, tuple(jnp.zeros((DV, ns), F32) for _ in range(nt)))
    for t in range(nt):
        ot_ref[t, pl.ds(base, DV), :] = accs[t]

    @pl.when(hd == HEADS_B - 1)
    def _():
        gavg = gavg_ref[...]
        for t in range(nt):
            r = slice(t * ns, (t + 1) * ns)
            o = ot_ref[t].T
            yb = _group_rms(o, gavg, gng_ref[...]) * _silu(z_ref[r, OFF_G:OFF_C])
            y_ref[r, W_A:W_A + W_B] = yb.astype(BF16)
        y = _dot(y_ref[...], wout_ref[...]).reshape(nt, ns, D)
        o_ref[...] = x_ref[...] + gt_ref[...] * _rms(y, gpost_ref[...])


def _mixs_call(layer, xs, mods, gpre, gpost, win_b, wout_b, alg, alb, acoef, abias, blb, gng,
               cw, cb, clg, clb, gavg, cst, s_in):
    nt, ns, _ = xs.shape
    row = lambda a: a.reshape(1, -1)
    mspec = lambda kk: pl.BlockSpec((None, ns, D), lambda h: (layer, 0, kk))
    hist = TAPS - 1
    return pl.pallas_call(
        functools.partial(_mixs_body, layer, nt),
        grid=(HEADS_B,),
        in_specs=[
            _full((nt, ns, D)),
            mspec(0), mspec(1), mspec(2),
            _full((1, D)), _full((1, D)),
            pl.BlockSpec((None, D, N_IN), lambda h: (layer, 0, 0)), pl.BlockSpec((None, D, D), lambda h: (layer, 0, 0)),
            _full((1, W_A)), _full((1, W_A)), _full(acoef.shape), _full(abias.shape),
            _full(blb.shape), _full((1, W_B)),
            _full((TAPS, W_C)), _full((1, W_C)), _full((1, W_C)), _full((1, W_C)),
            _full(gavg.shape), _full((hist, ns, W_C)),
            pl.BlockSpec((None, DK, DV, ns), lambda h: (h, 0, 0, 0)),
        ],
        out_specs=[
            _full((nt, ns, D)),
            pl.BlockSpec((None, DK, DV, ns), lambda h: (h, 0, 0, 0)),
            _full((nt, ns, W_C)),
            _full((nt, ns, W_A)),
        ],
        out_shape=[
            jax.ShapeDtypeStruct((nt, ns, D), F32),
            jax.ShapeDtypeStruct((HEADS_B, DK, DV, ns), F32),
            jax.ShapeDtypeStruct((nt, ns, W_C), F32),
            jax.ShapeDtypeStruct((nt, ns, W_A), F32),
        ],
        scratch_shapes=[
            pltpu.VMEM((nt * ns, N_IN), F32),
            pltpu.VMEM((nt * ns, D), BF16),
            pltpu.VMEM((nt, W_B, ns), F32),
            pltpu.VMEM((nt, W_B, ns), F32),
            pltpu.VMEM((nt, W_B, ns), F32),
            pltpu.VMEM((nt, W_B, ns), F32),
            pltpu.VMEM((nt, W_B, ns), F32),
        ],
        compiler_params=pltpu.CompilerParams(
            dimension_semantics=("arbitrary",), vmem_limit_bytes=VMEM_LIMIT),
        name=f"mix_sample_l{layer}",
    )(xs, mods, mods, mods, row(gpre), row(gpost), win_b, wout_b, row(alg), row(alb), acoef, abias,
      blb, row(gng), cw, row(cb), row(clg), row(clb), gavg, cst, s_in)


def _mlps_body(x_ref, sh_ref, sc_ref, gt_ref, gpre_ref, gpost_ref, wup_ref, wdn_ref, o_ref, h_ref, acc_ref):
    c = pl.program_id(0)
    nt, ns, _ = x_ref.shape

    @pl.when(c == 0)
    def _():
        h = _rms(x_ref[...], gpre_ref[...] * (1.0 + sc_ref[...])) + sh_ref[...]
        h_ref[...] = h.reshape(nt * ns, D).astype(BF16)
        acc_ref[...] = jnp.zeros_like(acc_ref)

    u = jnp.maximum(_dot(h_ref[...], wup_ref[...]), 0.0)
    acc_ref[...] += _dot((u * u).astype(BF16), wdn_ref[...])

    @pl.when(c == pl.num_programs(0) - 1)
    def _():
        o_ref[...] = x_ref[...] + _rms(acc_ref[...].reshape(nt, ns, D), gt_ref[...] * gpost_ref[...])


def _mlps_call(layer, x, mods, gpre, gpost, wup_b, wdn_b):
    nt, ns, _ = x.shape
    row = lambda v: v.reshape(1, -1)
    cw = D_FF // 4
    mspec = lambda kk: pl.BlockSpec((None, ns, D), lambda c: (layer, 0, kk))
    return pl.pallas_call(
        _mlps_body,
        grid=(D_FF // cw,),
        in_specs=[_full((nt, ns, D)), mspec(3), mspec(4), mspec(5), _full((1, D)), _full((1, D)),
                  pl.BlockSpec((None, D, cw), lambda c: (layer, 0, c)),
                  pl.BlockSpec((None, cw, D), lambda c: (layer, c, 0))],
        out_specs=_full((nt, ns, D)),
        out_shape=jax.ShapeDtypeStruct(x.shape, F32),
        scratch_shapes=[pltpu.VMEM((nt * ns, D), BF16), pltpu.VMEM((nt * ns, D), F32)],
        compiler_params=pltpu.CompilerParams(
            dimension_semantics=("arbitrary",), vmem_limit_bytes=VMEM_LIMIT),
        name=f"mlp_sample_l{layer}",
    )(x, mods, mods, mods, row(gpre), row(gpost), wup_b, wdn_b)


def kernel(x_prompt, x_sample, state_hgrn, state_conv, c_prompt, c_sample, w_ada, b_ada, g_pre_mix, g_post_mix, g_pre_mlp, g_post_mlp, w_in, a_ln_g, a_ln_b, a_w_s, a_b_s, b_lb, b_gn_g, c_w_dw, c_b_dw, c_ln_g, c_ln_b, w_out, w_up, w_down):
    depth = w_in.shape[0]
    nb = x_prompt.shape[0]
    ns, nt, _ = x_sample.shape
    mod_s, mod_p = _mod_call(c_sample, c_prompt, w_ada, b_ada)
    modp = mod_p.reshape(depth, nb, 1, 6 * D)
    masks = jnp.asarray(_level_masks(), dtype=BF16)
    lmask = jnp.asarray(_first_head_lanes(), dtype=BF16)
    bdm = jnp.asarray(_block_diag_mask())
    gavg = jnp.asarray(_group_avg(), dtype=BF16)
    win_b, wout_b, wup_b, wdn_b = (w.astype(BF16) for w in (w_in, w_out, w_up, w_down))

    xp = x_prompt
    xs = jnp.transpose(x_sample, (1, 0, 2))
    hg_p, hg_s, cv_p, cv_s, v_s = [], [], [], [], []
    for l in range(depth):
        wcat = jnp.transpose(a_w_s[l], (1, 0, 2)).reshape(CHUNK_A, HEADS_A * CHUNK_A)
        abias = jnp.repeat(a_b_s[l].T, DH_A, axis=1)
        acoef = jnp.repeat(jnp.transpose(a_w_s[l][:, :nt, :nt], (1, 2, 0)).reshape(nt * nt, HEADS_A),
                           DH_A, axis=1)

        xp, hs_l, cv_l = _layerp_call(
            l, xp, modp, g_pre_mix[l], g_post_mix[l], g_pre_mlp[l], g_post_mlp[l], win_b, wout_b, wup_b, wdn_b,
            a_ln_g[l], a_ln_b[l], wcat, abias, b_lb, b_gn_g[l], c_w_dw[l], c_b_dw[l], c_ln_g[l], c_ln_b[l],
            masks, lmask, bdm, gavg)

        cst = jnp.transpose(state_conv[l], (1, 0, 2))
        s_in = jnp.transpose(state_hgrn[l], (1, 2, 3, 0))
        xs, s_out, xg_s, vrow = _mixs_call(
            l, xs, mod_s, g_pre_mix[l], g_post_mix[l], win_b, wout_b, a_ln_g[l], a_ln_b[l], acoef,
            abias[:nt], b_lb, b_gn_g[l], c_w_dw[l], c_b_dw[l], c_ln_g[l], c_ln_b[l], gavg, cst, s_in)
        xs = _mlps_call(l, xs, mod_s, g_pre_mlp[l], g_post_mlp[l], wup_b, wdn_b)

        hg_p.append(hs_l)
        hg_s.append(jnp.transpose(s_out, (3, 0, 1, 2)))
        cv_p.append(cv_l)
        cv_s.append(jnp.concatenate([state_conv[l][:, nt:, :], jnp.transpose(xg_s, (1, 0, 2))], axis=1))
        v_s.append(jnp.transpose(vrow, (1, 0, 2)))

    return (xp, jnp.transpose(xs, (1, 0, 2)), jnp.stack(hg_p), jnp.stack(hg_s),
            jnp.stack(cv_p), jnp.stack(cv_s), jnp.stack(v_s))
```

```python
import functools

import numpy as np
import jax
import jax.numpy as jnp
from jax import lax
from jax.experimental import pallas as pl
from jax.experimental.pallas import tpu as pltpu

F32 = jnp.float32
BF16 = jnp.bfloat16

D = 1024
HEADS_A, DH_A, W_A, CHUNK_A = 4, 64, 256, 128
HEADS_B, DK, DV, W_B = 8, 64, 64, 512
SUB = 16
TILE_B = 128
W_C, TAPS = 256, 31
TAIL = 32
D_FF = 4096
N_IN = 3072
EPS = 1e-6
OFF_A, OFF_Q, OFF_F, OFF_I, OFF_G, OFF_C = 0, 512, 1024, 1536, 2048, 2560

TL = 512
MLP_CHUNKS = 8
VMEM_LIMIT = 60 * 1024 * 1024


def _dot(a, b):
    return jnp.dot(a, b, preferred_element_type=F32)


def _dot_nt(a, b):
    return lax.dot_general(a, b, (((1,), (1,)), ((), ())), preferred_element_type=F32)


def _rms(x, g):
    return x * lax.rsqrt(jnp.mean(x * x, axis=-1, keepdims=True) + EPS) * g


def _ln(x, g, b):
    xc = x - jnp.mean(x, axis=-1, keepdims=True)
    return xc * lax.rsqrt(jnp.mean(xc * xc, axis=-1, keepdims=True) + EPS) * g + b


def _silu(x):
    return x * jax.nn.sigmoid(x)


def _gelu(x):
    return 0.5 * x * (1.0 + lax.erf(x * 0.7071067811865476))


def _lower_bound(blb, layer):
    m = jnp.max(blb, axis=0, keepdims=True)
    e = jnp.exp(blb - m)
    tot = jnp.sum(e, axis=0, keepdims=True)
    acc = jnp.zeros_like(tot)
    for i in range(1, layer + 1):
        acc = acc + e[i:i + 1, :]
    return acc / tot


def _group_rms(o, gavg, g):
    ms = _dot((o * o).astype(BF16), gavg)
    return o * lax.rsqrt(ms + EPS) * g


def _mod_body(cs_ref, cp_ref, w_ref, b_ref, os_ref, op_ref):
    w = w_ref[...].astype(BF16)
    os_ref[...] = _dot(_silu(cs_ref[...]).astype(BF16), w) + b_ref[...]
    op_ref[...] = _dot(_silu(cp_ref[...]).astype(BF16), w) + b_ref[...]


def _mod_call(c_s, c_p, w_ada, b_ada):
    depth, _, n6 = w_ada.shape
    ns, nb = c_s.shape[0], c_p.shape[0]
    bn = 3072
    return pl.pallas_call(
        _mod_body,
        grid=(depth, n6 // bn),
        in_specs=[
            pl.BlockSpec((ns, D), lambda l, j: (0, 0)),
            pl.BlockSpec((nb, D), lambda l, j: (0, 0)),
            pl.BlockSpec((None, D, bn), lambda l, j: (l, 0, j)),
            pl.BlockSpec((None, 1, bn), lambda l, j: (l, 0, j)),
        ],
        out_specs=[pl.BlockSpec((None, ns, bn), lambda l, j: (l, 0, j)),
                   pl.BlockSpec((None, nb, bn), lambda l, j: (l, 0, j))],
        out_shape=[jax.ShapeDtypeStruct((depth, ns, n6), F32), jax.ShapeDtypeStruct((depth, nb, n6), F32)],
        compiler_params=pltpu.CompilerParams(
            dimension_semantics=("arbitrary", "arbitrary"), vmem_limit_bytes=VMEM_LIMIT),
        name="adaln_mod",
    )(c_s, c_p, w_ada, b_ada.reshape(depth, 1, n6))


def _gmlp_chunk(za, ln_g, ln_b, wcat_b, bias):
    z = _gelu(za)
    u = z[:, :W_A]
    v = _ln(z[:, W_A:], ln_g, ln_b)
    lane_h = lax.broadcasted_iota(jnp.int32, (CHUNK_A, W_A), 1) // DH_A
    vbd = jnp.concatenate(
        [jnp.where(lane_h == h, v, 0.0).astype(BF16) for h in range(HEADS_A)], axis=0)
    mixed = _dot(wcat_b, vbd) + bias
    return u * mixed


def _hgrn_tile(zq, zf, zi, zg, lb, gn_g, st_ref, masks_ref, lmask_ref, bd_ref, gavg, keep, fill):
    nch = TILE_B // SUB
    q = _silu(zq) * (DK ** -0.5)
    f = lb + (1.0 - lb) * jax.nn.sigmoid(zf)
    lf = jnp.log(f)
    k = 1.0 - f

    r15 = lax.broadcasted_iota(jnp.int32, (TILE_B, W_B), 0) & (SUB - 1)
    cs = lf
    for d in (1, 2, 4, 8):
        cs = cs + jnp.where(r15 >= d, pltpu.roll(cs, d, 0), 0.0)
    tot = [cs[SUB * c + SUB - 1:SUB * c + SUB, :] for c in range(nch)]
    tot_b = jnp.concatenate([jnp.broadcast_to(t, (SUB, W_B)) for t in tot], axis=0)
    q_in = q * jnp.exp(cs)
    k_out = k * jnp.exp(tot_b - cs)

    q0, k0 = [], []
    for c in range(nch):
        r = slice(SUB * c, SUB * c + SUB)
        if c % 2 == 0:
            q0.append(q[r] * jnp.exp(cs[r] - tot[c]))
            k0.append(k_out[r])
        else:
            q0.append(q_in[r])
            k0.append(k[r] * jnp.exp(-cs[r]))
    q0 = jnp.concatenate(q0, axis=0)
    k0 = jnp.concatenate(k0, axis=0)

    pre = [jnp.zeros((1, W_B), F32)]
    for c in range(nch):
        pre.append(pre[-1] + tot[c])

    def rows(fn):
        parts = []
        for c in range(nch):
            e = fn(c)
            parts.append(jnp.zeros((SUB, W_B), F32) if e is None
                         else jnp.broadcast_to(jnp.exp(e), (SUB, W_B)))
        return jnp.concatenate(parts, axis=0)

    a_st = rows(lambda c: pre[c])
    b_st = rows(lambda c: pre[nch] - pre[c + 1])
    a_64 = rows(lambda c: pre[c] - pre[4] if c >= 4 else None)
    b_64 = rows(lambda c: pre[4] - pre[c + 1] if c < 4 else None)
    a_32 = rows(lambda c: pre[c] - pre[c & ~3 | 2] if (c & 2) else None)
    b_32 = rows(lambda c: pre[c & ~3 | 2] - pre[c + 1] if not (c & 2) else None)
    decay = jnp.exp(pre[nch])

    m0 = lmask_ref[...] > 0.5
    zb = jnp.zeros((TILE_B, 128), BF16)

    def stack2(x):
        xb = x.astype(BF16)
        return jnp.concatenate([jnp.where(m0, xb, zb), jnp.where(m0, zb, xb)], axis=0)

    levels = ((q0, k0), (q_in * a_32, k_out * b_32), (q_in * a_64, k_out * b_64))
    q_st = q_in * a_st
    k_st = k_out * b_st
    v_t = zi.T
    bd = jnp.logical_and(bd_ref[...] > 0.5, keep)
    decay = jnp.where(keep, decay, 1.0)
    mk = [masks_ref[lv] > 0.5 for lv in range(len(levels))]

    fill()
    atts = []
    for p in range(HEADS_B // 2):
        sl = slice(128 * p, 128 * p + 128)
        att = jnp.zeros((TILE_B, 2 * TILE_B), BF16)
        for lv, (qa, ka) in reversed(list(enumerate(levels))):
            s = _dot_nt(qa[:, sl].astype(BF16), stack2(ka[:, sl]))
            att = jnp.where(mk[lv], s.astype(BF16), att)
        atts.append(att)
    fill()
    outs = []
    for p in range(HEADS_B // 2):
        sl = slice(128 * p, 128 * p + 128)
        o = _dot(atts[p], stack2(zi[:, sl]))
        st = st_ref[p]
        o = o + _dot_nt(q_st[:, sl].astype(BF16), st.astype(BF16))
        upd = _dot(v_t[sl, :].astype(BF16), k_st[:, sl].astype(BF16))
        st_ref[p] = st * decay[:, sl] + jnp.where(bd, upd, 0.0)
        outs.append(o)
    o = jnp.concatenate(outs, axis=1)
    fill()
    return _group_rms(o, gavg, gn_g) * _silu(zg)


def _layerp_body(layer, n_l, n_tiles,
                 x_ref, xn_ref, sh_ref, sc_ref, gt_ref, shn_ref, scn_ref, sh2_ref, sc2_ref, gt2_ref,
                 gpre_ref, gpost_ref, gpre2_ref, gpost2_ref, win_ref, wout_ref, wup_ref, wdn_ref,
                 alg_ref, alb_ref, wcat_ref, abias_ref, blb_ref, gng_ref,
                 cw_ref, cb_ref, clg_ref, clb_ref, masks_ref, lmask_ref, bd_ref, gavg_ref,
                 o_ref, hs_ref, cv_ref,
                 z_ref, xx_ref, xs_ref, y_ref, x1_ref, hm_ref, h_ref, st_ref, tail_ref):
    i = pl.program_id(0)
    keep = i < n_tiles

    @pl.when(jnp.logical_and(i % n_l == 0, keep))
    def _():
        st_ref[...] = jnp.zeros_like(st_ref)
        tail_ref[...] = jnp.zeros_like(tail_ref)

    @pl.when(i == 0)
    def _():
        x1_ref[1] = jnp.zeros((TL, D), F32)
        hm_ref[0] = jnp.zeros((TL, D), BF16)
        h_ref[0] = (_rms(x_ref[...], gpre_ref[...] * (1.0 + sc_ref[...])) + sh_ref[...]).astype(BF16)

    cwid = D_FF // MLP_CHUNKS
    slot, nslot = i % 2, (i + 1) % 2
    half = TL // 2
    rows = (slice(0, half), slice(half, TL))
    ym, pend = [None, None], {}

    def up(c, rh):
        u = jnp.maximum(_dot(hm_ref[slot, rows[rh], :], wup_ref[:, c * cwid:(c + 1) * cwid]), 0.0)
        pend[c, rh] = (u * u).astype(BF16)

    def down(c, rh):
        part = _dot(pend.pop((c, rh)), wdn_ref[c * cwid:(c + 1) * cwid, :])
        ym[rh] = part if ym[rh] is None else ym[rh] + part

    def close(rh):
        o_ref[rows[rh], :] = x1_ref[nslot, rows[rh], :] + _rms(ym[rh], gt2_ref[...] * gpost2_ref[...])

    units = [functools.partial(up, 0, 0), functools.partial(up, 0, 1)]
    for c in range(1, MLP_CHUNKS):
        for rh in range(2):
            units += [functools.partial(up, c, rh), functools.partial(down, c - 1, rh)]
    units += [functools.partial(down, MLP_CHUNKS - 1, 0), functools.partial(close, 0),
              functools.partial(down, MLP_CHUNKS - 1, 1), functools.partial(close, 1)]

    def fill(n=1):
        for _ in range(n):
            if units:
                units.pop(0)()

    z_ref[:, OFF_C:N_IN] = _dot(h_ref[slot], win_ref[:, OFF_C:N_IN])
    z_ref[:, 0:OFF_C] = _dot(h_ref[slot], win_ref[:, 0:OFF_C])
    h_ref[nslot] = (_rms(xn_ref[...], gpre_ref[...] * (1.0 + scn_ref[...])) + shn_ref[...]).astype(BF16)

    fill(2)

    xg = z_ref[:, OFF_C:OFF_C + W_C] * jax.nn.sigmoid(z_ref[:, OFF_C + W_C:N_IN])
    old_tail = tail_ref[...]
    xx_ref[0:TAIL, :] = old_tail
    xx_ref[TAIL:TAIL + TL, :] = xg
    tail_ref[...] = jnp.where(keep, xg[TL - TAIL:TL, :], old_tail)
    fill()
    span = TL + TAIL - 8
    for r in range(1, 8):
        xs_ref[r - 1] = xx_ref[pl.ds(r, span), :]
    fill()
    rb = 64
    for c in range(TL // rb):
        acc = jnp.zeros((rb, W_C), F32)
        for t in range(TAPS):
            a, r = divmod(TAIL - (TAPS - 1) + t, 8)
            src = xx_ref[pl.ds(c * rb + 8 * a, rb), :] if r == 0 else xs_ref[r - 1, pl.ds(c * rb + 8 * a, rb), :]
            acc = acc + src * cw_ref[pl.ds(t, 1), :]
        yc = _silu(_ln(acc + cb_ref[...], clg_ref[...], clb_ref[...]))
        y_ref[c * rb:(c + 1) * rb, W_A + W_B:D] = yc.astype(BF16)
        fill()

    lb = _lower_bound(blb_ref[...], layer)
    gavg = gavg_ref[...]
    for t in range(TL // TILE_B):
        r = slice(t * TILE_B, (t + 1) * TILE_B)
        yb = _hgrn_tile(z_ref[r, OFF_Q:OFF_F], z_ref[r, OFF_F:OFF_I], z_ref[r, OFF_I:OFF_G],
                        z_ref[r, OFF_G:OFF_C], lb, gng_ref[...], st_ref, masks_ref, lmask_ref, bd_ref, gavg,
                        keep, fill)
        y_ref[r, W_A:W_A + W_B] = yb.astype(BF16)

    tri = (lax.broadcasted_iota(jnp.int32, (CHUNK_A, HEADS_A * CHUNK_A), 1) & (CHUNK_A - 1)
           ) <= lax.broadcasted_iota(jnp.int32, (CHUNK_A, HEADS_A * CHUNK_A), 0)
    wcat_b = jnp.where(tri, wcat_ref[...], 0.0).astype(BF16)
    for c in range(TL // CHUNK_A):
        r = slice(c * CHUNK_A, (c + 1) * CHUNK_A)
        ya = _gmlp_chunk(z_ref[r, OFF_A:OFF_Q], alg_ref[...], alb_ref[...], wcat_b, abias_ref[...])
        y_ref[r, 0:W_A] = ya.astype(BF16)
        fill()

    fill(len(units) - 8)

    for rh in range(2):
        y = _dot(y_ref[rows[rh], :], wout_ref[...])
        fill(2)
        x1 = x_ref[rows[rh], :] + _rms(y, gt_ref[...] * gpost_ref[...])
        x1_ref[slot, rows[rh], :] = x1
        hm_ref[nslot, rows[rh], :] = (_rms(x1, gpre2_ref[...] * (1.0 + sc2_ref[...])) + sh2_ref[...]).astype(BF16)
    fill(len(units))

    @pl.when(jnp.logical_and(i % n_l == n_l - 1, keep))
    def _():
        for p in range(HEADS_B // 2):
            t = st_ref[p].T
            hs_ref[2 * p] = t[0:DK, 0:DV]
            hs_ref[2 * p + 1] = t[DK:2 * DK, DV:2 * DV]
        cv_ref[...] = tail_ref[TAIL - (TAPS - 1):TAIL, :]


def _level_masks():
    t = np.arange(TILE_B)[:, None]
    s = np.arange(2 * TILE_B)[None, :] % TILE_B
    ct, cs = t // SUB, s // SUB
    m = [
        (ct // 2 == cs // 2) & (s <= t),
        (ct // 2 == cs // 2 + 1) & ((ct // 2) % 2 == 1),
        (ct // 4 == cs // 4 + 1) & ((ct // 4) % 2 == 1),
    ]
    return np.stack(m).astype(np.float32)


def _first_head_lanes():
    return np.broadcast_to(np.arange(128)[None, :] < DK, (TILE_B, 128)).astype(np.float32)


def _block_diag_mask():
    i = np.arange(128)
    return ((i[:, None] // DK) == (i[None, :] // DK)).astype(np.float32)


def _group_avg():
    i = np.arange(W_B)
    return ((i[:, None] // DV) == (i[None, :] // DV)).astype(np.float32) / DV


def _full(shape):
    nd = len(shape)
    return pl.BlockSpec(shape, lambda *_: (0,) * nd)


def _layerp_call(layer, x, modp, gpre, gpost, gpre2, gpost2, win_b, wout_b, wup_b, wdn_b,
                 alg, alb, wcat, abias, blb, gng, cw, cb, clg, clb, masks, lmask, bdm, gavg):
    nb, L, _ = x.shape
    n_l = L // TL
    n_tiles = nb * n_l
    row = lambda a: a.reshape(1, -1)
    cur = lambda i: jnp.minimum(i, n_tiles - 1)
    prv = lambda i: jnp.maximum(i - 1, 0)
    nxt = lambda i: jnp.minimum(i + 1, n_tiles - 1)
    mcur = lambda kk: pl.BlockSpec((None, None, 1, D), lambda i: (layer, cur(i) // n_l, 0, kk))
    mprv = lambda kk: pl.BlockSpec((None, None, 1, D), lambda i: (layer, prv(i) // n_l, 0, kk))
    mnxt = lambda kk: pl.BlockSpec((None, None, 1, D), lambda i: (layer, nxt(i) // n_l, 0, kk))
    once = lambda shape: pl.BlockSpec((None,) + shape, lambda i: (layer,) + (0,) * len(shape),
                                      pipeline_mode=pl.Buffered(1))
    return pl.pallas_call(
        functools.partial(_layerp_body, layer, n_l, n_tiles),
        grid=(n_tiles + 1,),
        in_specs=[
            pl.BlockSpec((None, TL, D), lambda i: (cur(i) // n_l, cur(i) % n_l, 0)),
            pl.BlockSpec((None, TL, D), lambda i: (nxt(i) // n_l, nxt(i) % n_l, 0)),
            mcur(0), mcur(1), mcur(2), mnxt(0), mnxt(1), mcur(3), mcur(4), mprv(5),
            _full((1, D)), _full((1, D)), _full((1, D)), _full((1, D)),
            once((D, N_IN)), once((D, D)), once((D, D_FF)), once((D_FF, D)),
            _full((1, W_A)), _full((1, W_A)), _full((CHUNK_A, HEADS_A * CHUNK_A)), _full((CHUNK_A, W_A)),
            _full(blb.shape), _full((1, W_B)),
            _full((TAPS, W_C)), _full((1, W_C)), _full((1, W_C)), _full((1, W_C)),
            _full(masks.shape), _full(lmask.shape), _full(bdm.shape), _full(gavg.shape),
        ],
        out_specs=[
            pl.BlockSpec((None, TL, D), lambda i: (prv(i) // n_l, prv(i) % n_l, 0)),
            pl.BlockSpec((None, HEADS_B, DK, DV), lambda i: (cur(i) // n_l, 0, 0, 0)),
            pl.BlockSpec((None, TAPS - 1, W_C), lambda i: (cur(i) // n_l, 0, 0)),
        ],
        out_shape=[
            jax.ShapeDtypeStruct((nb, L, D), F32),
            jax.ShapeDtypeStruct((nb, HEADS_B, DK, DV), F32),
            jax.ShapeDtypeStruct((nb, TAPS - 1, W_C), F32),
        ],
        scratch_shapes=[
            pltpu.VMEM((TL, N_IN), F32),
            pltpu.VMEM((TAIL + TL, W_C), F32),
            pltpu.VMEM((7, TAIL + TL - 8, W_C), F32),
            pltpu.VMEM((TL, D), BF16),
            pltpu.VMEM((2, TL, D), F32),
            pltpu.VMEM((2, TL, D), BF16),
            pltpu.VMEM((2, TL, D), BF16),
            pltpu.VMEM((HEADS_B // 2, 128, 128), F32),
            pltpu.VMEM((TAIL, W_C), F32),
        ],
        compiler_params=pltpu.CompilerParams(
            dimension_semantics=("arbitrary",), vmem_limit_bytes=VMEM_LIMIT),
        name=f"layer_prompt_l{layer}",
    )(x, x, modp, modp, modp, modp, modp, modp, modp, modp, row(gpre), row(gpost), row(gpre2), row(gpost2),
      win_b, wout_b, wup_b, wdn_b, row(alg), row(alb), wcat, abias,
      blb, row(gng), cw, row(cb), row(clg), row(clb), masks, lmask, bdm, gavg)


def _mixs_body(layer, nt,
               x_ref, sh_ref, sc_ref, gt_ref, gpre_ref, gpost_ref, win_ref, wout_ref,
               alg_ref, alb_ref, acoef_ref, abias_ref, blb_ref, gng_ref,
               cw_ref, cb_ref, clg_ref, clb_ref, gavg_ref, cst_ref, s_ref,
               o_ref, so_ref, xg_ref, v_ref,
               z_ref, y_ref, ft_ref, kt_ref, qt_ref, vt_ref, ot_ref):
    hd = pl.program_id(0)
    ns = x_ref.shape[1]

    @pl.when(hd == 0)
    def _():
        x = x_ref[...]
        h = _rms(x, gpre_ref[...]) * (1.0 + sc_ref[...]) + sh_ref[...]
        z_ref[...] = _dot(h.reshape(nt * ns, D).astype(BF16), win_ref[...])

        za = _gelu(z_ref[:, OFF_A:OFF_Q])
        v = _ln(za[:, W_A:], alg_ref[...], alb_ref[...])
        for t in range(nt):
            mixed = abias_ref[pl.ds(t, 1), :]
            for s in range(t + 1):
                mixed = mixed + acoef_ref[pl.ds(t * nt + s, 1), :] * v[s * ns:(s + 1) * ns, :]
            y_ref[t * ns:(t + 1) * ns, 0:W_A] = (za[t * ns:(t + 1) * ns, :W_A] * mixed).astype(BF16)
            v_ref[t] = v[t * ns:(t + 1) * ns, :]

        xg = z_ref[:, OFF_C:OFF_C + W_C] * jax.nn.sigmoid(z_ref[:, OFF_C + W_C:N_IN])
        hist = TAPS - 1
        for t in range(nt):
            xg_ref[t] = xg[t * ns:(t + 1) * ns, :]
        for t in range(nt):
            acc = jnp.zeros((ns, W_C), F32)
            for tap in range(TAPS):
                pos = t + tap
                src = cst_ref[pos] if pos < hist else xg[(pos - hist) * ns:(pos - hist + 1) * ns, :]
                acc = acc + src * cw_ref[pl.ds(tap, 1), :]
            yc = _silu(_ln(acc + cb_ref[...], clg_ref[...], clb_ref[...]))
            y_ref[t * ns:(t + 1) * ns, W_A + W_B:D] = yc.astype(BF16)

        lb = _lower_bound(blb_ref[...], layer)
        for t in range(nt):
            r = slice(t * ns, (t + 1) * ns)
            f = lb + (1.0 - lb) * jax.nn.sigmoid(z_ref[r, OFF_F:OFF_I])
            ft_ref[t] = f.T
            kt_ref[t] = (1.0 - f).T
            qt_ref[t] = (_silu(z_ref[r, OFF_Q:OFF_F]) * (DK ** -0.5)).T
            vt_ref[t] = z_ref[r, OFF_I:OFF_G].T

    base = pl.multiple_of(hd * DK, DK)
    vts = [vt_ref[t, pl.ds(base, DV), :] for t in range(nt)]

    def krow(i, accs):
        srow = s_ref[i]
        new = []
        for t in range(nt):
            srow = ft_ref[t, pl.ds(base + i, 1), :] * srow + kt_ref[t, pl.ds(base + i, 1), :] * vts[t]
            new.append(accs[t] + qt_ref[t, pl.ds(base + i, 1), :] * srow)
        so_ref[i] = srow
        return tuple(new)

    accs = lax.fori_loop(0, DK, krow, tuple(jnp.zeros((DV, ns), F32) for _ in range(nt)))
    for t in range(nt):
        ot_ref[t, pl.ds(base, DV), :] = accs[t]

    @pl.when(hd == HEADS_B - 1)
    def _():
        gavg = gavg_ref[...]
        for t in range(nt):
            r = slice(t * ns, (t + 1) * ns)
            o = ot_ref[t].T
            yb = _group_rms(o, gavg, gng_ref[...]) * _silu(z_ref[r, OFF_G:OFF_C])
            y_ref[r, W_A:W_A + W_B] = yb.astype(BF16)
        y = _dot(y_ref[...], wout_ref[...]).reshape(nt, ns, D)
        o_ref[...] = x_ref[...] + gt_ref[...] * _rms(y, gpost_ref[...])


def _mixs_call(layer, xs, mods, gpre, gpost, win_b, wout_b, alg, alb, acoef, abias, blb, gng,
               cw, cb, clg, clb, gavg, cst, s_in):
    nt, ns, _ = xs.shape
    row = lambda a: a.reshape(1, -1)
    mspec = lambda kk: pl.BlockSpec((None, ns, D), lambda h: (layer, 0, kk))
    hist = TAPS - 1
    return pl.pallas_call(
        functools.partial(_mixs_body, layer, nt),
        grid=(HEADS_B,),
        in_specs=[
            _full((nt, ns, D)),
            mspec(0), mspec(1), mspec(2),
            _full((1, D)), _full((1, D)),
            pl.BlockSpec((None, D, N_IN), lambda h: (layer, 0, 0)), pl.BlockSpec((None, D, D), lambda h: (layer, 0, 0)),
            _full((1, W_A)), _full((1, W_A)), _full(acoef.shape), _full(abias.shape),
            _full(blb.shape), _full((1, W_B)),
            _full((TAPS, W_C)), _full((1, W_C)), _full((1, W_C)), _full((1, W_C)),
            _full(gavg.shape), _full((hist, ns, W_C)),
            pl.BlockSpec((None, DK, DV, ns), lambda h: (h, 0, 0, 0)),
        ],
        out_specs=[
            _full((nt, ns, D)),
            pl.BlockSpec((None, DK, DV, ns), lambda h: (h, 0, 0, 0)),
            _full((nt, ns, W_C)),
            _full((nt, ns, W_A)),
        ],
        out_shape=[
            jax.ShapeDtypeStruct((nt, ns, D), F32),
            jax.ShapeDtypeStruct((HEADS_B, DK, DV, ns), F32),
            jax.ShapeDtypeStruct((nt, ns, W_C), F32),
            jax.ShapeDtypeStruct((nt, ns, W_A), F32),
        ],
        scratch_shapes=[
            pltpu.VMEM((nt * ns, N_IN), F32),
            pltpu.VMEM((nt * ns, D), BF16),
            pltpu.VMEM((nt, W_B, ns), F32),
            pltpu.VMEM((nt, W_B, ns), F32),
            pltpu.VMEM((nt, W_B, ns), F32),
            pltpu.VMEM((nt, W_B, ns), F32),
            pltpu.VMEM((nt, W_B, ns), F32),
        ],
        compiler_params=pltpu.CompilerParams(
            dimension_semantics=("arbitrary",), vmem_limit_bytes=VMEM_LIMIT),
        name=f"mix_sample_l{layer}",
    )(xs, mods, mods, mods, row(gpre), row(gpost), win_b, wout_b, row(alg), row(alb), acoef, abias,
      blb, row(gng), cw, row(cb), row(clg), row(clb), gavg, cst, s_in)


def _mlps_body(x_ref, sh_ref, sc_ref, gt_ref, gpre_ref, gpost_ref, wup_ref, wdn_ref, o_ref, h_ref, acc_ref):
    c = pl.program_id(0)
    nt, ns, _ = x_ref.shape

    @pl.when(c == 0)
    def _():
        h = _rms(x_ref[...], gpre_ref[...] * (1.0 + sc_ref[...])) + sh_ref[...]
        h_ref[...] = h.reshape(nt * ns, D).astype(BF16)
        acc_ref[...] = jnp.zeros_like(acc_ref)

    u = jnp.maximum(_dot(h_ref[...], wup_ref[...]), 0.0)
    acc_ref[...] += _dot((u * u).astype(BF16), wdn_ref[...])

    @pl.when(c == pl.num_programs(0) - 1)
    def _():
        o_ref[...] = x_ref[...] + _rms(acc_ref[...].reshape(nt, ns, D), gt_ref[...] * gpost_ref[...])


def _mlps_call(layer, x, mods, gpre, gpost, wup_b, wdn_b):
    nt, ns, _ = x.shape
    row = lambda v: v.reshape(1, -1)
    cw = D_FF // 4
    mspec = lambda kk: pl.BlockSpec((None, ns, D), lambda c: (layer, 0, kk))
    return pl.pallas_call(
        _mlps_body,
        grid=(D_FF // cw,),
        in_specs=[_full((nt, ns, D)), mspec(3), mspec(4), mspec(5), _full((1, D)), _full((1, D)),
                  pl.BlockSpec((None, D, cw), lambda c: (layer, 0, c)),
                  pl.BlockSpec((None, cw, D), lambda c: (layer, c, 0))],
        out_specs=_full((nt, ns, D)),
        out_shape=jax.ShapeDtypeStruct(x.shape, F32),
        scratch_shapes=[pltpu.VMEM((nt * ns, D), BF16), pltpu.VMEM((nt * ns, D), F32)],
        compiler_params=pltpu.CompilerParams(
            dimension_semantics=("arbitrary",), vmem_limit_bytes=VMEM_LIMIT),
        name=f"mlp_sample_l{layer}",
    )(x, mods, mods, mods, row(gpre), row(gpost), wup_b, wdn_b)


def kernel(x_prompt, x_sample, state_hgrn, state_conv, c_prompt, c_sample, w_ada, b_ada, g_pre_mix, g_post_mix, g_pre_mlp, g_post_mlp, w_in, a_ln_g, a_ln_b, a_w_s, a_b_s, b_lb, b_gn_g, c_w_dw, c_b_dw, c_ln_g, c_ln_b, w_out, w_up, w_down):
    depth = w_in.shape[0]
    nb = x_prompt.shape[0]
    ns, nt, _ = x_sample.shape
    mod_s, mod_p = _mod_call(c_sample, c_prompt, w_ada, b_ada)
    modp = mod_p.reshape(depth, nb, 1, 6 * D)
    masks = jnp.asarray(_level_masks(), dtype=BF16)
    lmask = jnp.asarray(_first_head_lanes(), dtype=BF16)
    bdm = jnp.asarray(_block_diag_mask())
    gavg = jnp.asarray(_group_avg(), dtype=BF16)
    win_b, wout_b, wup_b, wdn_b = (w.astype(BF16) for w in (w_in, w_out, w_up, w_down))

    xp = x_prompt
    xs = jnp.transpose(x_sample, (1, 0, 2))
    hg_p, hg_s, cv_p, cv_s, v_s = [], [], [], [], []
    for l in range(depth):
        wcat = jnp.transpose(a_w_s[l], (1, 0, 2)).reshape(CHUNK_A, HEADS_A * CHUNK_A)
        abias = jnp.repeat(a_b_s[l].T, DH_A, axis=1)
        acoef = jnp.repeat(jnp.transpose(a_w_s[l][:, :nt, :nt], (1, 2, 0)).reshape(nt * nt, HEADS_A),
                           DH_A, axis=1)

        xp, hs_l, cv_l = _layerp_call(
            l, xp, modp, g_pre_mix[l], g_post_mix[l], g_pre_mlp[l], g_post_mlp[l], win_b, wout_b, wup_b, wdn_b,
            a_ln_g[l], a_ln_b[l], wcat, abias, b_lb, b_gn_g[l], c_w_dw[l], c_b_dw[l], c_ln_g[l], c_ln_b[l],
            masks, lmask, bdm, gavg)

        cst = jnp.transpose(state_conv[l], (1, 0, 2))
        s_in = jnp.transpose(state_hgrn[l], (1, 2, 3, 0))
        xs, s_out, xg_s, vrow = _mixs_call(
            l, xs, mod_s, g_pre_mix[l], g_post_mix[l], win_b, wout_b, a_ln_g[l], a_ln_b[l], acoef,
            abias[:nt], b_lb, b_gn_g[l], c_w_dw[l], c_b_dw[l], c_ln_g[l], c_ln_b[l], gavg, cst, s_in)
        xs = _mlps_call(l, xs, mod_s, g_pre_mlp[l], g_post_mlp[l], wup_b, wdn_b)

        hg_p.append(hs_l)
        hg_s.append(jnp.transpose(s_out, (3, 0, 1, 2)))
        cv_p.append(cv_l)
        cv_s.append(jnp.concatenate([state_conv[l][:, nt:, :], jnp.transpose(xg_s, (1, 0, 2))], axis=1))
        v_s.append(jnp.transpose(vrow, (1, 0, 2)))

    return (xp, jnp.transpose(xs, (1, 0, 2)), jnp.stack(hg_p), jnp.stack(hg_s),
            jnp.stack(cv_p), jnp.stack(cv_s), jnp.stack(v_s))
```
